```python
import math
import jax, jax.numpy as jnp
from jax import lax
import numpy as np

D_MODEL = 4096
BATCH = 4
SEQ = 2048
DEPTH = 2
DEC_BATCH = 8
DEC_SEQ = 1
PAST_LEN = 16384
PAGE_SIZE = 128

HEAD_DIM = 128
ATT_W = D_MODEL // 2
N_HEADS = ATT_W // HEAD_DIM
CONV_C = D_MODEL - ATT_W
CONV_K = 31
PLE_DIM = 256
DILATED_PAIRS = ((128, 1), (512, 4), (2048, 16))
WIN_MAX = max(w for w, _ in DILATED_PAIRS)
BLOCK = 128
N_BUCKETS = 32
MAX_DISTANCE = 2048
EPS = 1e-6
NEG = -1e30
SPLIT_SIZES = (ATT_W, ATT_W, ATT_W, ATT_W, CONV_C, CONV_C, CONV_C)
SPLIT_IDX = [int(s) for s in np.cumsum(SPLIT_SIZES)[:-1]]
IN_COLS = int(sum(SPLIT_SIZES))

kernel_name = "hymba_dilated_conformer_decoder_step"


def rmsnorm(x, g):
    xf = x.astype(jnp.float32)
    r = xf * lax.rsqrt(jnp.mean(xf * xf, axis=-1, keepdims=True) + EPS)
    return (r * g.astype(jnp.float32)).astype(x.dtype)


def layernorm(x, g, b):
    xf = x.astype(jnp.float32)
    mu = jnp.mean(xf, axis=-1, keepdims=True)
    var = jnp.mean(jnp.square(xf - mu), axis=-1, keepdims=True)
    y = (xf - mu) * lax.rsqrt(var + EPS) * g.astype(jnp.float32) + b.astype(jnp.float32)
    return y.astype(x.dtype)


def rel_bucket(dist):
    max_exact = N_BUCKETS // 2
    df = jnp.maximum(dist, 1).astype(jnp.float32)
    large = max_exact + (jnp.log(df / max_exact) / math.log(MAX_DISTANCE / max_exact)
                         * (N_BUCKETS - max_exact)).astype(jnp.int32)
    large = jnp.minimum(large, N_BUCKETS - 1)
    return jnp.where(dist < max_exact, dist, large)


def dilated_band_prompt(q, k, v, rel_bias, dil, band):
    B, S, H, E = q.shape
    L = S // dil
    Lp = -(-L // BLOCK) * BLOCK
    nb = Lp // BLOCK

    def to_sub(x):
        x = x.reshape(B, L, dil, H, E).transpose(0, 2, 1, 3, 4)
        return jnp.pad(x, ((0, 0), (0, 0), (0, Lp - L), (0, 0), (0, 0)))

    qs, ks, vs = to_sub(q), to_sub(k), to_sub(v)
    qb = qs.reshape(B, dil, nb, BLOCK, H, E)

    def band_keys(x):
        xp = jnp.pad(x, ((0, 0), (0, 0), (BLOCK, 0), (0, 0), (0, 0)))
        prev = xp[:, :, :Lp].reshape(B, dil, nb, BLOCK, H, E)
        cur = x.reshape(B, dil, nb, BLOCK, H, E)
        return jnp.concatenate([prev, cur], axis=3)

    kb, vb = band_keys(ks), band_keys(vs)
    qi = jnp.arange(BLOCK)[:, None]
    kj = jnp.arange(2 * BLOCK)[None, :]
    delta = qi + BLOCK - kj
    in_band = (delta >= 0) & (delta <= band)
    key_ok = (jnp.arange(nb)[:, None] * BLOCK + kj - BLOCK) >= 0
    mask = in_band[None] & key_ok[:, None, :]
    bias = rel_bias[rel_bucket(jnp.maximum(delta, 0) * dil)].astype(jnp.float32)
    bias = bias.transpose(2, 0, 1)

    s = jnp.einsum('bdnqhe,bdnkhe->bdnhqk', qb, kb).astype(jnp.float32) + bias
    s = jnp.where(mask[None, None, :, None], s, NEG)
    lse = jax.nn.logsumexp(s, axis=-1)
    pr = jnp.exp(s - lse[..., None]).astype(v.dtype)
    o = jnp.einsum('bdnhqk,bdnkhe->bdnqhe', pr, vb)
    o = o.reshape(B, dil, Lp, H, E)[:, :, :L].transpose(0, 2, 1, 3, 4).reshape(B, S, H, E)
    lse = lse.transpose(0, 1, 2, 4, 3).reshape(B, dil, Lp, H)[:, :, :L]
    lse = lse.transpose(0, 2, 1, 3).reshape(B, S, H)
    return o, lse


def dilated_gather_sample(q, kcat, vcat, rel_bias, dil, band, n_past):
    T = q.shape[1]
    i = jnp.arange(T)[:, None]
    j = jnp.arange(band + 1)[None, :]
    idx = n_past + i - j * dil
    valid = idx >= 0
    idxc = jnp.maximum(idx, 0)
    kg = kcat[:, idxc]
    vg = vcat[:, idxc]
    bias = rel_bias[rel_bucket(j * dil)].astype(jnp.float32)
    s = jnp.einsum('bthe,btjhe->bthj', q, kg).astype(jnp.float32) + bias.transpose(0, 2, 1)[None]
    s = jnp.where(valid[None, :, None, :], s, NEG)
    lse = jax.nn.logsumexp(s, axis=-1)
    pr = jnp.exp(s - lse[..., None]).astype(vcat.dtype)
    o = jnp.einsum('bthj,btjhe->bthe', pr, vg)
    return o, lse


def merge_by_denominator(outs):
    lse = jnp.stack([l for _, l in outs], axis=0)
    w = jax.nn.softmax(lse, axis=0)
    o = jnp.stack([o.astype(jnp.float32) for o, _ in outs], axis=0)
    return jnp.sum(w[..., None] * o, axis=0)


def attend_prompt(q, k, v, rel_bias):
    outs = [dilated_band_prompt(q, k, v, rel_bias, d, w // d) for (w, d) in DILATED_PAIRS]
    o = merge_by_denominator(outs).astype(v.dtype)
    keep = min(WIN_MAX, q.shape[1])
    return o, k[:, -keep:], v[:, -keep:]


def attend_sample(q, k, v, ck, cv, rel_bias):
    n_past = ck.shape[1]
    kcat = jnp.concatenate([ck.astype(k.dtype), k], axis=1)
    vcat = jnp.concatenate([cv.astype(v.dtype), v], axis=1)
    outs = [dilated_gather_sample(q, kcat, vcat, rel_bias, d, w // d, n_past) for (w, d) in DILATED_PAIRS]
    o = merge_by_denominator(outs).astype(v.dtype)
    keep = min(WIN_MAX, kcat.shape[1])
    return o, kcat[:, -keep:], vcat[:, -keep:]


def conv_module(glu_a, glu_b, gate_b, left, conv_w, conv_b, ln_g, ln_b):
    u = glu_a * jax.nn.sigmoid(glu_b)
    ucat = jnp.concatenate([left.astype(u.dtype), u], axis=1)
    y = lax.conv_general_dilated(ucat, conv_w.astype(u.dtype)[:, None, :], window_strides=(1,),
                                 padding='VALID', dimension_numbers=('NWC', 'WIO', 'NWC'),
                                 feature_group_count=u.shape[-1]) + conv_b
    y = jax.nn.silu(layernorm(y, ln_g, ln_b)) * jax.nn.silu(gate_b)
    return y, ucat[:, -(CONV_K - 1):]


def decoder_layer(h, p_l, attn_fn, conv_left, g_pre, w_in, conv_w, conv_b, ln_g, ln_b,
                  w_out, g_post, w_ple, g_ple, w_pg, b_pg):
    B, T, _ = h.shape
    xn = rmsnorm(h, g_pre)
    u = jnp.einsum('btd,dc->btc', xn, w_in)
    q, k, v, gate_a, glu_a, glu_b, gate_b = jnp.split(u, SPLIT_IDX, axis=-1)
    q = q.reshape(B, T, N_HEADS, HEAD_DIM) * (HEAD_DIM ** -0.5)
    k = k.reshape(B, T, N_HEADS, HEAD_DIM)
    v = v.reshape(B, T, N_HEADS, HEAD_DIM)
    o, k_state, v_state = attn_fn(q, k, v)
    y_a = o.reshape(B, T, ATT_W) * jax.nn.silu(gate_a)
    y_b, conv_state = conv_module(glu_a, glu_b, gate_b, conv_left, conv_w, conv_b, ln_g, ln_b)
    y = jnp.einsum('btc,cd->btd', jnp.concatenate([y_a, y_b], axis=-1), w_out)
    h = h + rmsnorm(y, g_post)
    gate = jax.nn.sigmoid(jnp.einsum('btd,de->bte', h, w_pg) + b_pg)
    h = h + gate * rmsnorm(jnp.einsum('btp,pd->btd', p_l, w_ple), g_ple)
    return h, k_state, v_state, conv_state


def setup_inputs(seed: int = 0) -> dict:
    key = jax.random.key(seed)
    ks = jax.random.split(key, 24)
    f32 = jnp.float32
    nrm = lambda k, shape, s: jax.random.normal(k, shape, f32) * s
    lw = min(WIN_MAX, PAST_LEN)
    return {
        "x_prompt": nrm(ks[0], (BATCH, SEQ, D_MODEL), 1.0),
        "x_sample": nrm(ks[1], (DEC_BATCH, DEC_SEQ, D_MODEL), 1.0),
        "cache_k": nrm(ks[2], (DEPTH, DEC_BATCH, lw, N_HEADS, HEAD_DIM), 1.0),
        "cache_v": nrm(ks[3], (DEPTH, DEC_BATCH, lw, N_HEADS, HEAD_DIM), 1.0),
        "state_conv": nrm(ks[4], (DEPTH, DEC_BATCH, CONV_K - 1, CONV_C), 0.5),
        "p_prompt": nrm(ks[5], (DEPTH, BATCH, SEQ, PLE_DIM), 1.0),
        "p_sample": nrm(ks[6], (DEPTH, DEC_BATCH, DEC_SEQ, PLE_DIM), 1.0),
        "rel_bias": nrm(ks[7], (N_BUCKETS, N_HEADS), 0.5),
        "g_pre": 1.0 + nrm(ks[8], (DEPTH, D_MODEL), 0.02),
        "w_in": nrm(ks[9], (DEPTH, D_MODEL, IN_COLS), D_MODEL ** -0.5),
        "conv_w": nrm(ks[10], (DEPTH, CONV_K, CONV_C), CONV_K ** -0.5),
        "conv_b": nrm(ks[11], (DEPTH, CONV_C), 0.01),
        "ln_g": 1.0 + nrm(ks[12], (DEPTH, CONV_C), 0.02),
        "ln_b": nrm(ks[13], (DEPTH, CONV_C), 0.01),
        "w_out": nrm(ks[14], (DEPTH, ATT_W + CONV_C, D_MODEL), (ATT_W + CONV_C) ** -0.5),
        "g_post": 1.0 + nrm(ks[15], (DEPTH, D_MODEL), 0.02),
        "w_ple": nrm(ks[16], (DEPTH, PLE_DIM, D_MODEL), PLE_DIM ** -0.5),
        "g_ple": 1.0 + nrm(ks[17], (DEPTH, D_MODEL), 0.02),
        "w_pg": nrm(ks[18], (DEPTH, D_MODEL, D_MODEL), D_MODEL ** -0.5),
        "b_pg": nrm(ks[19], (DEPTH, D_MODEL), 0.01),
    }


def reference(x_prompt, x_sample, cache_k, cache_v, state_conv, p_prompt, p_sample, rel_bias,
              g_pre, w_in, conv_w, conv_b, ln_g, ln_b, w_out, g_post, w_ple, g_ple, w_pg, b_pg):
    hp, hs = x_prompt, x_sample
    kp_l, vp_l, cp_l, ks_l, vs_l, cs_l = [], [], [], [], [], []
    zero_left = jnp.zeros((x_prompt.shape[0], CONV_K - 1, CONV_C), x_prompt.dtype)
    for l in range(DEPTH):
        wl = (g_pre[l], w_in[l], conv_w[l], conv_b[l], ln_g[l], ln_b[l],
              w_out[l], g_post[l], w_ple[l], g_ple[l], w_pg[l], b_pg[l])
        ap = lambda q, k, v: attend_prompt(q, k, v, rel_bias)
        hp, kp, vp, cp = decoder_layer(hp, p_prompt[l], ap, zero_left, *wl)
        ck, cv = cache_k[l], cache_v[l]
        asmp = lambda q, k, v, ck=ck, cv=cv: attend_sample(q, k, v, ck, cv, rel_bias)
        hs, ksn, vsn, csn = decoder_layer(hs, p_sample[l], asmp, state_conv[l], *wl)
        kp_l.append(kp); vp_l.append(vp); cp_l.append(cp)
        ks_l.append(ksn); vs_l.append(vsn); cs_l.append(csn)
    new_k_prompt = jnp.stack(kp_l, axis=0)
    new_v_prompt = jnp.stack(vp_l, axis=0)
    new_conv_prompt = jnp.stack(cp_l, axis=0)
    new_k_sample = jnp.stack(ks_l, axis=0)
    new_v_sample = jnp.stack(vs_l, axis=0)
    new_conv_sample = jnp.stack(cs_l, axis=0)
    return (hp, hs, new_k_prompt, new_v_prompt, new_conv_prompt, new_k_sample, new_v_sample, new_conv_sample)
```

```python
import functools
import math

import jax
import jax.numpy as jnp
from jax import lax
from jax.experimental import pallas as pl
from jax.experimental.pallas import tpu as pltpu

D_MODEL = 4096
HEAD_DIM = 128
ATT_W = 2048
N_HEADS = 16
CONV_C = 2048
CONV_K = 31
PLE_DIM = 256
N_BUCKETS = 32
MAX_DISTANCE = 2048
WIN_MAX = 2048
EPS = 1e-6
NEG = -1e30
DEPTH = 2

F32 = jnp.float32
BF16 = jnp.bfloat16

VMEM_LIMIT = 56 * 1024 * 1024
SAMPLE_ROWS = 16
LANE = 128
QBLK = 128
ATT_CHUNK = 256
TAB_W = WIN_MAX + QBLK


def _cparams(sem):
    return pltpu.CompilerParams(dimension_semantics=sem, vmem_limit_bytes=VMEM_LIMIT)


def _rmsnorm_kernel(x_ref, g_ref, o_ref):
    x = x_ref[...]
    r = x * lax.rsqrt(jnp.mean(x * x, axis=-1, keepdims=True) + EPS)
    o_ref[...] = (r * g_ref[...]).astype(o_ref.dtype)


def _rmsnorm(x, g, tm):
    m, d = x.shape
    return pl.pallas_call(
        _rmsnorm_kernel,
        grid=(m // tm,),
        in_specs=[pl.BlockSpec((tm, d), lambda i: (i, 0)), pl.BlockSpec((1, d), lambda i: (0, 0))],
        out_specs=pl.BlockSpec((tm, d), lambda i: (i, 0)),
        out_shape=jax.ShapeDtypeStruct((m, d), BF16),
        compiler_params=_cparams(("parallel",)),
        name="rmsnorm",
    )(x, g.reshape(1, d))


def _mm_kernel(*refs, nparts, scale):
    xs = refs[:nparts]
    ss = refs[nparts:2 * nparts]
    ws = refs[2 * nparts:3 * nparts]
    o_ref, os_ref = refs[3 * nparts:3 * nparts + 2]
    wbs = refs[3 * nparts + 2:]

    @pl.when(pl.program_id(1) == 0)
    def _():
        acc = None
        for s_ref, w_ref, wb_ref in zip(ss, ws, wbs):
            wb_ref[...] = w_ref[...].astype(BF16)
            d = jnp.dot(s_ref[...], wb_ref[...], preferred_element_type=F32)
            acc = d if acc is None else acc + d
        os_ref[...] = (acc * scale).astype(os_ref.dtype)

    acc = None
    for x_ref, wb_ref in zip(xs, wbs):
        d = jnp.dot(x_ref[...], wb_ref[...], preferred_element_type=F32)
        acc = d if acc is None else acc + d
    o_ref[...] = (acc * scale).astype(o_ref.dtype)


def _mm(x_parts, s_parts, w, col0, ncols, *, scale=1.0, out_dtype=F32, tm=1024, tn=512, name="mm"):
    nparts = len(x_parts)
    m = x_parts[0].shape[0]
    ms = s_parts[0].shape[0]
    kparts = [x.shape[1] for x in x_parts]
    assert sum(kparts) == w.shape[0] and col0 % tn == 0 and ncols % tn == 0 and m % tm == 0
    jb0 = col0 // tn
    koff = [sum(kparts[:p]) for p in range(nparts)]
    in_specs = (
        [pl.BlockSpec((tm, kp), lambda j, i: (i, 0)) for kp in kparts]
        + [pl.BlockSpec((ms, kp), lambda j, i: (0, 0)) for kp in kparts]
        + [pl.BlockSpec((kp, tn), functools.partial(lambda j, i, kb: (kb, jb0 + j), kb=ko // kp))
           for kp, ko in zip(kparts, koff)]
    )
    return pl.pallas_call(
        functools.partial(_mm_kernel, nparts=nparts, scale=scale),
        grid=(ncols // tn, m // tm),
        in_specs=in_specs,
        out_specs=[pl.BlockSpec((tm, tn), lambda j, i: (i, j)), pl.BlockSpec((ms, tn), lambda j, i: (0, j))],
        out_shape=[jax.ShapeDtypeStruct((m, ncols), out_dtype), jax.ShapeDtypeStruct((ms, ncols), out_dtype)],
        scratch_shapes=[pltpu.VMEM((kp, tn), BF16) for kp in kparts],
        compiler_params=_cparams(("parallel", "arbitrary")),
        name=name,
    )(*x_parts, *s_parts, *([w] * nparts))


def _gate_kernel(x_ref, s_ref, w_ref, b_ref, h_ref, hs_ref, pe_ref, pes_ref, o_ref, os_ref, wb_ref):
    @pl.when(pl.program_id(1) == 0)
    def _():
        wb_ref[...] = w_ref[...].astype(BF16)
        acc = jnp.dot(s_ref[...], wb_ref[...], preferred_element_type=F32) + b_ref[...]
        os_ref[...] = hs_ref[...] + jax.nn.sigmoid(acc) * pes_ref[...]

    acc = jnp.dot(x_ref[...], wb_ref[...], preferred_element_type=F32) + b_ref[...]
    o_ref[...] = h_ref[...] + jax.nn.sigmoid(acc) * pe_ref[...]


def _gate_mm(x, s, w, b, h, hs, pe, pes, *, tm=1024, tn=512):
    m, k = x.shape
    ms = s.shape[0]
    n = w.shape[1]
    big = pl.BlockSpec((tm, tn), lambda j, i: (i, j))
    small = pl.BlockSpec((ms, tn), lambda j, i: (0, j))
    return pl.pallas_call(
        _gate_kernel,
        grid=(n // tn, m // tm),
        in_specs=[pl.BlockSpec((tm, k), lambda j, i: (i, 0)), pl.BlockSpec((ms, k), lambda j, i: (0, 0)),
                  pl.BlockSpec((k, tn), lambda j, i: (0, j)), pl.BlockSpec((1, tn), lambda j, i: (0, j)),
                  big, small, big, small],
        out_specs=[big, small],
        out_shape=[jax.ShapeDtypeStruct((m, n), F32), jax.ShapeDtypeStruct((ms, n), F32)],
        scratch_shapes=[pltpu.VMEM((k, tn), BF16)],
        compiler_params=_cparams(("parallel", "arbitrary")),
        name="gate_mm",
    )(x, s, w, b.reshape(1, n), h, hs, pe, pes)


def _post_kernel(h_ref, y_ref, p_ref, gpost_ref, wple_ref, gple_ref, h1_ref, h1b_ref, pen_ref):
    y = y_ref[...]
    r = y * lax.rsqrt(jnp.mean(y * y, axis=-1, keepdims=True) + EPS) * gpost_ref[...]
    h1 = h_ref[...] + r
    h1_ref[...] = h1
    h1b_ref[...] = h1.astype(BF16)
    pe = jnp.dot(p_ref[...].astype(BF16), wple_ref[...], preferred_element_type=F32)
    pen_ref[...] = pe * lax.rsqrt(jnp.mean(pe * pe, axis=-1, keepdims=True) + EPS) * gple_ref[...]


def _post(h, y, p, g_post, w_ple_bf, g_ple, tm):
    m, d = h.shape
    row = pl.BlockSpec((tm, d), lambda i: (i, 0))
    vec = pl.BlockSpec((1, d), lambda i: (0, 0))
    return pl.pallas_call(
        _post_kernel,
        grid=(m // tm,),
        in_specs=[row, row, pl.BlockSpec((tm, PLE_DIM), lambda i: (i, 0)), vec,
                  pl.BlockSpec((PLE_DIM, d), lambda i: (0, 0)), vec],
        out_specs=[row, row, row],
        out_shape=[jax.ShapeDtypeStruct((m, d), F32), jax.ShapeDtypeStruct((m, d), BF16),
                   jax.ShapeDtypeStruct((m, d), F32)],
        compiler_params=_cparams(("parallel",)),
        name="post_norm",
    )(h, y, p, g_post.reshape(1, d), w_ple_bf, g_ple.reshape(1, d))


def _attn_kernel(q_ref, k_ref, v_ref, g_ref, tab_ref, o_ref, kb_ref, vb_ref, s_ref, *, seq):
    kb_ref[...] = k_ref[...].astype(BF16)
    vb_ref[...] = v_ref[...].astype(BF16)
    nblk = seq // QBLK
    for i in range(nblk):
        q = q_ref[i * QBLK:(i + 1) * QBLK, :]
        nkeys = (i + 1) * QBLK
        tab0 = (nblk - i) * QBLK
        chunks = [(c, min(ATT_CHUNK, nkeys - c)) for c in range(0, nkeys, ATT_CHUNK)]
        mx = jnp.full((QBLK, LANE), NEG, F32)
        for c, w in chunks:
            s = lax.dot_general(q, kb_ref[c:c + w, :], (((1,), (1,)), ((), ())), preferred_element_type=F32)
            s = s + tab_ref[:, tab0 + c:tab0 + c + w]
            s_ref[:, c:c + w] = s
            for t in range(0, w, LANE):
                mx = jnp.maximum(mx, s[:, t:t + LANE])
        m = jnp.max(mx, axis=-1, keepdims=True)
        ls = jnp.zeros((QBLK, LANE), F32)
        acc = jnp.zeros((QBLK, HEAD_DIM), F32)
        for c, w in chunks:
            p = jnp.exp(s_ref[:, c:c + w] - m)
            for t in range(0, w, LANE):
                ls = ls + p[:, t:t + LANE]
            acc = acc + jnp.dot(p.astype(BF16), vb_ref[c:c + w, :], preferred_element_type=F32)
        l = jnp.sum(ls, axis=-1, keepdims=True)
        g = g_ref[i * QBLK:(i + 1) * QBLK, :]
        o = acc * (1.0 / l)
        o_ref[i * QBLK:(i + 1) * QBLK, :] = (o * (g * jax.nn.sigmoid(g))).astype(o_ref.dtype)


def _attn_prompt(q, k, v, u4, tab, batch, seq):
    m = q.shape[0]
    blk = lambda h, b: (b, h)
    return pl.pallas_call(
        functools.partial(_attn_kernel, seq=seq),
        grid=(N_HEADS, batch),
        in_specs=[pl.BlockSpec((seq, HEAD_DIM), blk), pl.BlockSpec((seq, HEAD_DIM), blk),
                  pl.BlockSpec((seq, HEAD_DIM), blk), pl.BlockSpec((seq, HEAD_DIM), blk),
                  pl.BlockSpec((None, QBLK, TAB_W), lambda h, b: (h, 0, 0))],
        out_specs=pl.BlockSpec((seq, HEAD_DIM), blk),
        out_shape=jax.ShapeDtypeStruct((m, ATT_W), BF16),
        scratch_shapes=[pltpu.VMEM((seq, HEAD_DIM), BF16), pltpu.VMEM((seq, HEAD_DIM), BF16),
                        pltpu.VMEM((QBLK, seq), F32)],
        compiler_params=_cparams(("parallel", "parallel")),
        name="attn_prompt",
    )(q, k, v, u4, tab)


CONV_TB = 256
CONV_HIST = 32
CONV_RC = 32
NCH = CONV_C // LANE


def _conv_kernel(ga_ref, gb_ref, gate_ref, w_ref, cb_ref, lg_ref, lb_ref, yb_ref, st_ref, ucat_ref, y_ref):
    t = pl.program_id(1)
    tb = CONV_TB

    @pl.when(t == 0)
    def _():
        ucat_ref[:, 0:CONV_HIST, :] = jnp.zeros((NCH, CONV_HIST, LANE), F32)

    @pl.when(t > 0)
    def _():
        ucat_ref[:, 0:CONV_HIST, :] = ucat_ref[:, tb:tb + CONV_HIST, :]

    for c in range(NCH):
        sl = slice(c * LANE, (c + 1) * LANE)
        ga = ga_ref[:, sl]
        ucat_ref[c, CONV_HIST:CONV_HIST + tb, :] = ga * jax.nn.sigmoid(gb_ref[:, sl])

    off = CONV_HIST - (CONV_K - 1)

    def chunk(c, carry):
        wc = w_ref[c]
        bias = cb_ref[c]
        for r in range(0, tb, CONV_RC):
            acc = jnp.broadcast_to(bias, (CONV_RC, LANE))
            for k in range(CONV_K):
                acc = acc + wc[k:k + 1, :] * ucat_ref[c, r + off + k:r + off + k + CONV_RC, :]
            y_ref[c, r:r + CONV_RC, :] = acc
        return carry

    lax.fori_loop(0, NCH, chunk, 0)

    ssum = jnp.zeros((tb, LANE), F32)
    for c in range(NCH):
        ssum = ssum + y_ref[c]
    mu = jnp.sum(ssum, axis=-1, keepdims=True) * (1.0 / CONV_C)
    sq = jnp.zeros((tb, LANE), F32)
    for c in range(NCH):
        d = y_ref[c] - mu
        sq = sq + d * d
    rstd = lax.rsqrt(jnp.sum(sq, axis=-1, keepdims=True) * (1.0 / CONV_C) + EPS)
    for c in range(NCH):
        sl = slice(c * LANE, (c + 1) * LANE)
        z = (y_ref[c] - mu) * rstd * lg_ref[c] + lb_ref[c]
        g = gate_ref[:, sl]
        yb_ref[:, sl] = (z * jax.nn.sigmoid(z) * (g * jax.nn.sigmoid(g))).astype(yb_ref.dtype)

    @pl.when(t == pl.num_programs(1) - 1)
    def _():
        for c in range(NCH):
            st_ref[:, c * LANE:(c + 1) * LANE] = ucat_ref[c, CONV_HIST + tb - (CONV_K - 1):CONV_HIST + tb, :]


def _chunked(v):
    return v.reshape(v.shape[0], NCH, LANE).transpose(1, 0, 2)


def _conv_prompt(u4, conv_w, conv_b, ln_g, ln_b, batch, seq):
    m = u4.shape[0]
    nt = seq // CONV_TB
    wpad = jnp.zeros((CONV_HIST, CONV_C), F32).at[:CONV_K].set(conv_w)
    col = lambda j: pl.BlockSpec((CONV_TB, CONV_C), lambda b, t, j=j: (b * nt + t, j))
    vec = pl.BlockSpec((NCH, 1, LANE), lambda b, t: (0, 0, 0))
    return pl.pallas_call(
        _conv_kernel,
        grid=(batch, nt),
        in_specs=[col(1), col(2), col(3), pl.BlockSpec((NCH, CONV_HIST, LANE), lambda b, t: (0, 0, 0)), vec, vec, vec],
        out_specs=[pl.BlockSpec((CONV_TB, CONV_C), lambda b, t: (b * nt + t, 0)),
                   pl.BlockSpec((None, CONV_K - 1, CONV_C), lambda b, t: (b, 0, 0))],
        out_shape=[jax.ShapeDtypeStruct((m, CONV_C), BF16),
                   jax.ShapeDtypeStruct((batch, CONV_K - 1, CONV_C), F32)],
        scratch_shapes=[pltpu.VMEM((NCH, CONV_HIST + CONV_TB, LANE), F32), pltpu.VMEM((NCH, CONV_TB, LANE), F32)],
        compiler_params=_cparams(("parallel", "arbitrary")),
        name="conv_prompt",
    )(u4, u4, u4, _chunked(wpad), _chunked(conv_b.reshape(1, -1)), _chunked(ln_g.reshape(1, -1)),
      _chunked(ln_b.reshape(1, -1)))


def _conv_s_kernel(u4_ref, st_ref, w_ref, cb_ref, lg_ref, lb_ref, yb_ref, so_ref, *, nb):
    ga = u4_ref[:, CONV_C:2 * CONV_C]
    u = ga * jax.nn.sigmoid(u4_ref[:, 2 * CONV_C:3 * CONV_C])
    gate = u4_ref[:, 3 * CONV_C:4 * CONV_C]
    yb_ref[...] = jnp.zeros(yb_ref.shape, yb_ref.dtype)
    for b in range(nb):
        st = st_ref[b]
        un = u[b:b + 1, :]
        y = (jnp.sum(st * w_ref[0:CONV_K - 1, :], axis=0, keepdims=True)
             + un * w_ref[CONV_K - 1:CONV_K, :] + cb_ref[...])
        mu = jnp.mean(y, axis=-1, keepdims=True)
        d = y - mu
        z = d * lax.rsqrt(jnp.mean(d * d, axis=-1, keepdims=True) + EPS) * lg_ref[...] + lb_ref[...]
        g = gate[b:b + 1, :]
        yb_ref[b:b + 1, :] = (z * jax.nn.sigmoid(z) * (g * jax.nn.sigmoid(g))).astype(yb_ref.dtype)
        so_ref[b, 0:CONV_K - 2, :] = st[1:CONV_K - 1, :]
        so_ref[b, CONV_K - 2:CONV_K - 1, :] = un


def _conv_sample(u4s, state, conv_w, conv_b, ln_g, ln_b):
    nb = state.shape[0]
    return pl.pallas_call(
        functools.partial(_conv_s_kernel, nb=nb),
        out_shape=[jax.ShapeDtypeStruct((SAMPLE_ROWS, CONV_C), F32),
                   jax.ShapeDtypeStruct(state.shape, F32)],
        compiler_params=pltpu.CompilerParams(vmem_limit_bytes=VMEM_LIMIT),
        name="conv_sample",
    )(u4s, state, conv_w, conv_b.reshape(1, -1), ln_g.reshape(1, -1), ln_b.reshape(1, -1))


def _attn_s_kernel(q_ref, kn_ref, vn_ref, g_ref, k1_ref, k2_ref, k3_ref, v1_ref, v2_ref, v3_ref,
                   tab_ref, tnew_ref, o_ref):
    q = q_ref[...]
    head_of_lane = lax.broadcasted_iota(jnp.int32, (N_HEADS, ATT_W), 1) // HEAD_DIM
    head_of_row = lax.broadcasted_iota(jnp.int32, (N_HEADS, ATT_W), 0)
    own = head_of_lane == head_of_row
    qbd = jnp.where(own, jnp.broadcast_to(q, (N_HEADS, ATT_W)), 0.0)
    qbd_bf = qbd.astype(BF16)
    nt = (((1,), (1,)), ((), ()))
    s = [lax.dot_general(qbd_bf, kr[...].astype(BF16), nt, preferred_element_type=F32) + tab_ref[p]
         for p, kr in enumerate((k1_ref, k2_ref, k3_ref))]
    s_new = jnp.sum(qbd * kn_ref[...], axis=-1, keepdims=True) + tnew_ref[...]
    m = s_new
    for sp in s:
        m = jnp.maximum(m, jnp.max(sp, axis=-1, keepdims=True))
    p_new = jnp.exp(s_new - m)
    l = p_new
    acc = p_new * vn_ref[...]
    for sp, vr in zip(s, (v1_ref, v2_ref, v3_ref)):
        p = jnp.exp(sp - m)
        l = l + jnp.sum(p, axis=-1, keepdims=True)
        acc = acc + jnp.dot(p.astype(BF16), vr[...].astype(BF16), preferred_element_type=F32)
    o = jnp.sum(jnp.where(own, acc * (1.0 / l), 0.0), axis=0, keepdims=True)
    g = g_ref[...]
    o_ref[...] = (o * (g * jax.nn.sigmoid(g))).astype(o_ref.dtype)


def _attn_sample(qs, ks, vs, u4s, ck, cv, tab_s, tab_new):
    nb = ck.shape[0]
    row = lambda w: pl.BlockSpec((None, 1, w), lambda b: (b, 0, 0))
    c1 = lambda c: c.reshape(nb, WIN_MAX // 128, 128, ATT_W)
    c2 = lambda c: c.reshape(nb, WIN_MAX // 4, 4 * ATT_W)
    c3 = lambda c: c.reshape(nb, WIN_MAX // 16, 16 * ATT_W)
    s1 = pl.BlockSpec((None, None, 128, ATT_W), lambda b: (b, WIN_MAX // 128 - 1, 0, 0))
    s2 = pl.BlockSpec((None, 128, ATT_W), lambda b: (b, 3, 0))
    s3 = pl.BlockSpec((None, 128, ATT_W), lambda b: (b, 0, 0))
    return pl.pallas_call(
        _attn_s_kernel,
        grid=(nb,),
        in_specs=[row(ATT_W), row(ATT_W), row(ATT_W), row(ATT_W), s1, s2, s3, s1, s2, s3,
                  pl.BlockSpec((3, N_HEADS, 128), lambda b: (0, 0, 0)),
                  pl.BlockSpec((N_HEADS, 1), lambda b: (0, 0))],
        out_specs=row(ATT_W),
        out_shape=jax.ShapeDtypeStruct((nb, 1, ATT_W), F32),
        compiler_params=_cparams(("parallel",)),
        name="attn_sample",
    )(qs, ks, vs, u4s, c1(ck), c2(ck), c3(ck), c1(cv), c2(cv), c3(cv), tab_s, tab_new)


def _rel_bucket(dist):
    max_exact = N_BUCKETS // 2
    df = jnp.maximum(dist, 1).astype(F32)
    large = max_exact + (jnp.log(df / max_exact) / math.log(MAX_DISTANCE / max_exact)
                         * (N_BUCKETS - max_exact)).astype(jnp.int32)
    large = jnp.minimum(large, N_BUCKETS - 1)
    return jnp.where(dist < max_exact, dist, large)


def _bias_tables(rel_bias):
    dist = jnp.arange(WIN_MAX + 1, dtype=jnp.int32)
    bias = rel_bias[_rel_bucket(dist)].astype(F32).T
    mult = ((dist <= 128).astype(F32) + ((dist % 4 == 0) & (dist <= 512)).astype(F32)
            + (dist % 16 == 0).astype(F32))
    t1d = jnp.where(mult > 0, bias + jnp.log(jnp.maximum(mult, 1.0)), NEG)
    r = jnp.arange(QBLK, dtype=jnp.int32)[:, None]
    c = jnp.arange(TAB_W, dtype=jnp.int32)[None, :]
    dm = WIN_MAX + r - c
    ok = (dm >= 0) & (dm <= WIN_MAX)
    tab = jnp.where(ok[None], t1d[:, jnp.clip(dm, 0, WIN_MAX)], NEG)
    j = jnp.arange(128, dtype=jnp.int32)
    d_rows = jnp.stack([128 - j, 512 - 4 * j, 2048 - 16 * j])
    tab_s = bias[:, d_rows].transpose(1, 0, 2)
    tab_new = (bias[:, 0] + math.log(3.0)).reshape(N_HEADS, 1)
    return tab, tab_s, tab_new


def kernel(x_prompt, x_sample, cache_k, cache_v, state_conv, p_prompt, p_sample, rel_bias, g_pre, w_in, conv_w,
           conv_b, ln_g, ln_b, w_out, g_post, w_ple, g_ple, w_pg, b_pg):
    batch, seq, d = x_prompt.shape
    nb = x_sample.shape[0]
    m = batch * seq
    pad = lambda a: jnp.zeros((SAMPLE_ROWS, a.shape[-1]), a.dtype).at[:nb].set(a.reshape(nb, -1))

    hp = x_prompt.reshape(m, d)
    hs = pad(x_sample)
    tab, tab_s, tab_new = _bias_tables(rel_bias)
    kp_l, vp_l, cp_l, ks_l, vs_l, cs_l = [], [], [], [], [], []
    for l in range(DEPTH):
        xn = _rmsnorm(hp, g_pre[l], 256)
        xns = _rmsnorm(hs, g_pre[l], SAMPLE_ROWS)
        q, qs = _mm([xn], [xns], w_in[l], 0, ATT_W, scale=HEAD_DIM ** -0.5, out_dtype=BF16, name="proj_q")
        k, ks = _mm([xn], [xns], w_in[l], ATT_W, ATT_W, name="proj_k")
        v, vs = _mm([xn], [xns], w_in[l], 2 * ATT_W, ATT_W, name="proj_v")
        u4, u4s = _mm([xn], [xns], w_in[l], 3 * ATT_W, ATT_W + 3 * CONV_C, name="proj_rest")

        ya = _attn_prompt(q, k, v, u4, tab, batch, seq)
        yb, cstate = _conv_prompt(u4, conv_w[l], conv_b[l], ln_g[l], ln_b[l], batch, seq)

        r3 = lambda a, w: a[:nb, :w].reshape(nb, 1, w)
        yas = _attn_sample(r3(qs.astype(F32), ATT_W), r3(ks, ATT_W), r3(vs, ATT_W), r3(u4s, ATT_W),
                           cache_k[l].reshape(nb, WIN_MAX, ATT_W), cache_v[l].reshape(nb, WIN_MAX, ATT_W),
                           tab_s, tab_new)
        ybs, cstate_s = _conv_sample(u4s, state_conv[l], conv_w[l], conv_b[l], ln_g[l], ln_b[l])

        y, ys = _mm([ya, yb], [pad(yas).astype(BF16), ybs.astype(BF16)], w_out[l], 0, d, name="proj_out")
        wple_bf = w_ple[l].astype(BF16)
        h1, h1b, pen = _post(hp, y, p_prompt[l].reshape(m, PLE_DIM), g_post[l], wple_bf, g_ple[l], 128)
        h1s, h1bs, pens = _post(hs, ys, pad(p_sample[l]), g_post[l], wple_bf, g_ple[l], SAMPLE_ROWS)
        hp, hs = _gate_mm(h1b, h1bs, w_pg[l], b_pg[l], h1, h1s, pen, pens)

        kp_l.append(k.reshape(batch, seq, N_HEADS, HEAD_DIM))
        vp_l.append(v.reshape(batch, seq, N_HEADS, HEAD_DIM))
        cp_l.append(cstate)
        knew = ks[:nb].reshape(nb, 1, N_HEADS, HEAD_DIM)
        vnew = vs[:nb].reshape(nb, 1, N_HEADS, HEAD_DIM)
        ks_l.append(jnp.concatenate([cache_k[l][:, 1:], knew], axis=1))
        vs_l.append(jnp.concatenate([cache_v[l][:, 1:], vnew], axis=1))
        cs_l.append(cstate_s)

    return (hp.reshape(batch, seq, d), hs[:nb].reshape(nb, 1, d),
            jnp.stack(kp_l), jnp.stack(vp_l), jnp.stack(cp_l),
            jnp.stack(ks_l), jnp.stack(vs_l), jnp.stack(cs_l))
```

```python
import functools
import math

import jax
import jax.numpy as jnp
from jax import lax
from jax.experimental import pallas as pl
from jax.experimental.pallas import tpu as pltpu

D_MODEL = 4096
HEAD_DIM = 128
ATT_W = 2048
N_HEADS = 16
CONV_C = 2048
CONV_K = 31
PLE_DIM = 256
N_BUCKETS = 32
MAX_DISTANCE = 2048
WIN_MAX = 2048
EPS = 1e-6
NEG = -1e30
DEPTH = 2

F32 = jnp.float32
BF16 = jnp.bfloat16

VMEM_LIMIT = 56 * 1024 * 1024
SAMPLE_ROWS = 16
LANE = 128
QBLK = 128
ATT_CHUNK = 256
TAB_W = WIN_MAX + QBLK
TAB_P = TAB_W + QBLK


def _cparams(sem):
    return pltpu.CompilerParams(dimension_semantics=sem, vmem_limit_bytes=VMEM_LIMIT)


def _rmsnorm_kernel(x_ref, g_ref, o_ref):
    x = x_ref[...]
    r = x * lax.rsqrt(jnp.mean(x * x, axis=-1, keepdims=True) + EPS)
    o_ref[...] = (r * g_ref[...]).astype(o_ref.dtype)


def _rmsnorm(x, g, tm):
    m, d = x.shape
    return pl.pallas_call(
        _rmsnorm_kernel,
        grid=(m // tm,),
        in_specs=[pl.BlockSpec((tm, d), lambda i: (i, 0)), pl.BlockSpec((1, d), lambda i: (0, 0))],
        out_specs=pl.BlockSpec((tm, d), lambda i: (i, 0)),
        out_shape=jax.ShapeDtypeStruct((m, d), BF16),
        compiler_params=_cparams(("parallel",)),
        name="rmsnorm",
    )(x, g.reshape(1, d))


def _mm_kernel(*refs, nparts, scale, has_prev):
    xs = refs[:nparts]
    ss = refs[nparts:2 * nparts]
    ws = refs[2 * nparts:3 * nparts]
    no = 3 * nparts + (1 if has_prev else 0)
    o_ref, os_ref = refs[no:no + 2]
    wbs = refs[no + 2:]

    @pl.when(pl.program_id(1) == 0)
    def _():
        acc = None
        for s_ref, w_ref, wb_ref in zip(ss, ws, wbs):
            wb_ref[...] = w_ref[...].astype(BF16)
            d = jnp.dot(s_ref[...], wb_ref[...], preferred_element_type=F32)
            acc = d if acc is None else acc + d
        os_ref[...] = (acc * scale).astype(os_ref.dtype)

    acc = None
    for x_ref, wb_ref in zip(xs, wbs):
        d = jnp.dot(x_ref[...], wb_ref[...], preferred_element_type=F32)
        acc = d if acc is None else acc + d
    res = (acc * scale).astype(o_ref.dtype)
    if has_prev:
        o_ref[0] = refs[3 * nparts][...]
        o_ref[1] = res
    else:
        o_ref[...] = res


def _mm(x_parts, s_parts, w, col0, ncols, *, scale=1.0, out_dtype=F32, tm=1024, tn=512, name="mm", prev=None):
    nparts = len(x_parts)
    m = x_parts[0].shape[0]
    ms = s_parts[0].shape[0]
    kparts = [x.shape[1] for x in x_parts]
    assert sum(kparts) == w.shape[0] and col0 % tn == 0 and ncols % tn == 0 and m % tm == 0
    jb0 = col0 // tn
    koff = [sum(kparts[:p]) for p in range(nparts)]
    in_specs = (
        [pl.BlockSpec((tm, kp), lambda j, i: (i, 0)) for kp in kparts]
        + [pl.BlockSpec((ms, kp), lambda j, i: (0, 0)) for kp in kparts]
        + [pl.BlockSpec((kp, tn), functools.partial(lambda j, i, kb: (kb, jb0 + j), kb=ko // kp))
           for kp, ko in zip(kparts, koff)]
    )
    args = [*x_parts, *s_parts, *([w] * nparts)]
    if prev is None:
        big_spec = pl.BlockSpec((tm, tn), lambda j, i: (i, j))
        big_shape = jax.ShapeDtypeStruct((m, ncols), out_dtype)
    else:
        assert DEPTH == 2 and prev.shape == (m, ncols) and prev.dtype == out_dtype
        in_specs.append(pl.BlockSpec((tm, tn), lambda j, i: (i, j)))
        args.append(prev)
        big_spec = pl.BlockSpec((DEPTH, tm, tn), lambda j, i: (0, i, j))
        big_shape = jax.ShapeDtypeStruct((DEPTH, m, ncols), out_dtype)
    return pl.pallas_call(
        functools.partial(_mm_kernel, nparts=nparts, scale=scale, has_prev=prev is not None),
        grid=(ncols // tn, m // tm),
        in_specs=in_specs,
        out_specs=[big_spec, pl.BlockSpec((ms, tn), lambda j, i: (0, j))],
        out_shape=[big_shape, jax.ShapeDtypeStruct((ms, ncols), out_dtype)],
        scratch_shapes=[pltpu.VMEM((kp, tn), BF16) for kp in kparts],
        compiler_params=_cparams(("parallel", "arbitrary")),
        name=name,
    )(*args)


def _gate_kernel(x_ref, s_ref, w_ref, b_ref, h_ref, hs_ref, pe_ref, pes_ref, o_ref, os_ref, wb_ref):
    @pl.when(pl.program_id(1) == 0)
    def _():
        wb_ref[...] = w_ref[...].astype(BF16)
        acc = jnp.dot(s_ref[...], wb_ref[...], preferred_element_type=F32) + b_ref[...]
        os_ref[...] = hs_ref[...] + jax.nn.sigmoid(acc) * pes_ref[...]

    acc = jnp.dot(x_ref[...], wb_ref[...], preferred_element_type=F32) + b_ref[...]
    o_ref[...] = h_ref[...] + jax.nn.sigmoid(acc) * pe_ref[...]


def _gate_mm(x, s, w, b, h, hs, pe, pes, *, tm=1024, tn=512):
    m, k = x.shape
    ms = s.shape[0]
    n = w.shape[1]
    big = pl.BlockSpec((tm, tn), lambda j, i: (i, j))
    small = pl.BlockSpec((ms, tn), lambda j, i: (0, j))
    return pl.pallas_call(
        _gate_kernel,
        grid=(n // tn, m // tm),
        in_specs=[pl.BlockSpec((tm, k), lambda j, i: (i, 0)), pl.BlockSpec((ms, k), lambda j, i: (0, 0)),
                  pl.BlockSpec((k, tn), lambda j, i: (0, j)), pl.BlockSpec((1, tn), lambda j, i: (0, j)),
                  big, small, big, small],
        out_specs=[big, small],
        out_shape=[jax.ShapeDtypeStruct((m, n), F32), jax.ShapeDtypeStruct((ms, n), F32)],
        scratch_shapes=[pltpu.VMEM((k, tn), BF16)],
        compiler_params=_cparams(("parallel", "arbitrary")),
        name="gate_mm",
    )(x, s, w, b.reshape(1, n), h, hs, pe, pes)


def _post_kernel(h_ref, y_ref, p_ref, gpost_ref, wple_ref, gple_ref, h1_ref, h1b_ref, pen_ref):
    y = y_ref[...]
    r = y * lax.rsqrt(jnp.mean(y * y, axis=-1, keepdims=True) + EPS) * gpost_ref[...]
    h1 = h_ref[...] + r
    h1_ref[...] = h1
    h1b_ref[...] = h1.astype(BF16)
    pe = jnp.dot(p_ref[...].astype(BF16), wple_ref[...], preferred_element_type=F32)
    pen_ref[...] = pe * lax.rsqrt(jnp.mean(pe * pe, axis=-1, keepdims=True) + EPS) * gple_ref[...]


def _post(h, y, p, g_post, w_ple_bf, g_ple, tm):
    m, d = h.shape
    row = pl.BlockSpec((tm, d), lambda i: (i, 0))
    vec = pl.BlockSpec((1, d), lambda i: (0, 0))
    return pl.pallas_call(
        _post_kernel,
        grid=(m // tm,),
        in_specs=[row, row, pl.BlockSpec((tm, PLE_DIM), lambda i: (i, 0)), vec,
                  pl.BlockSpec((PLE_DIM, d), lambda i: (0, 0)), vec],
        out_specs=[row, row, row],
        out_shape=[jax.ShapeDtypeStruct((m, d), F32), jax.ShapeDtypeStruct((m, d), BF16),
                   jax.ShapeDtypeStruct((m, d), F32)],
        compiler_params=_cparams(("parallel",)),
        name="post_norm",
    )(h, y, p, g_post.reshape(1, d), w_ple_bf, g_ple.reshape(1, d))


def _cache_copies(ck_ref, cv_ref, kn_ref, vn_ref, ok_ref, ov_ref, sem, layer, bulk_layers):
    nb, win = ck_ref.shape[1], ck_ref.shape[2]
    cps = []

    def add(src, dst):
        cps.append(pltpu.make_async_copy(src, dst, sem.at[len(cps)]))

    for lb in bulk_layers:
        for b in range(nb):
            for c_ref, o_ref in ((ck_ref, ok_ref), (cv_ref, ov_ref)):
                add(c_ref.at[lb, b, pl.ds(1, win - 1)], o_ref.at[lb, b, pl.ds(0, win - 1)])
                if lb != layer:
                    add(c_ref.at[lb, b, pl.ds(win - 1, 1)], o_ref.at[lb, b, pl.ds(win - 1, 1)])
    for b in range(nb):
        add(kn_ref.at[pl.ds(b, 1)], ok_ref.at[layer, b, pl.ds(win - 1, 1)])
        add(vn_ref.at[pl.ds(b, 1)], ov_ref.at[layer, b, pl.ds(win - 1, 1)])
    return cps


def _n_cache_copies(nb, layer, bulk_layers):
    return sum(2 * nb * (1 if lb == layer else 2) for lb in bulk_layers) + 2 * nb


def _attn_kernel(*refs, seq, layer, bulk_layers, has_prev):
    q_ref, k_ref, v_ref, g_ref, trow_ref, ck_ref, cv_ref, kn_ref, vn_ref = refs[:9]
    no = 9 + (2 if has_prev else 0)
    o_ref, ok_ref, ov_ref, kb_ref, vb_ref, s_ref, tab_ref, sem = refs[no:]
    hid, bid = pl.program_id(0), pl.program_id(1)
    cps = _cache_copies(ck_ref, cv_ref, kn_ref, vn_ref, ok_ref, ov_ref, sem, layer, bulk_layers)

    @pl.when((hid == 0) & (bid == 0))
    def _():
        for cp in cps:
            cp.start()

    @pl.when(bid == 0)
    def _():
        tab_ref[...] = pltpu.roll(jnp.broadcast_to(trow_ref[...], (QBLK, TAB_P)), 0, 1, stride=1, stride_axis=0)

    kb_ref[...] = k_ref[...].astype(BF16)
    vb_ref[...] = v_ref[...].astype(BF16)
    nblk = seq // QBLK
    for i in range(nblk):
        q = q_ref[i * QBLK:(i + 1) * QBLK, :]
        nkeys = (i + 1) * QBLK
        tab0 = (nblk - i) * QBLK
        chunks = [(c, min(ATT_CHUNK, nkeys - c)) for c in range(0, nkeys, ATT_CHUNK)]
        mx = jnp.full((QBLK, LANE), NEG, F32)
        for c, w in chunks:
            s = lax.dot_general(q, kb_ref[c:c + w, :], (((1,), (1,)), ((), ())), preferred_element_type=F32)
            s = s + tab_ref[:, tab0 + c:tab0 + c + w]
            s_ref[:, c:c + w] = s
            for t in range(0, w, LANE):
                mx = jnp.maximum(mx, s[:, t:t + LANE])
        m = jnp.max(mx, axis=-1, keepdims=True)
        ls = jnp.zeros((QBLK, LANE), F32)
        acc = jnp.zeros((QBLK, HEAD_DIM), F32)
        for c, w in chunks:
            p = jnp.exp(s_ref[:, c:c + w] - m)
            for t in range(0, w, LANE):
                ls = ls + p[:, t:t + LANE]
            acc = acc + jnp.dot(p.astype(BF16), vb_ref[c:c + w, :], preferred_element_type=F32)
        l = jnp.sum(ls, axis=-1, keepdims=True)
        g = g_ref[i * QBLK:(i + 1) * QBLK, :]
        o = acc * (1.0 / l)
        o_ref[i * QBLK:(i + 1) * QBLK, :] = (o * (g * jax.nn.sigmoid(g))).astype(o_ref.dtype)

    @pl.when((hid == pl.num_programs(0) - 1) & (bid == pl.num_programs(1) - 1))
    def _():
        for cp in cps:
            cp.wait()


def _attn_prompt(q, kst, vst, u4, trow, cache_k, cache_v, knew, vnew, prev_k, prev_v, layer, batch, seq):
    m = q.shape[0]
    blk = lambda h, b: (b, h)
    if kst.ndim == 3:
        kv = pl.BlockSpec((None, seq, HEAD_DIM), lambda h, b: (layer, b, h))
    else:
        kv = pl.BlockSpec((seq, HEAD_DIM), blk)
    anyspec = pl.BlockSpec(memory_space=pl.ANY)
    has_prev = prev_k is not None
    bulk_layers = () if has_prev else tuple(range(DEPTH))
    args = [q, kst, vst, u4, trow, cache_k, cache_v, knew, vnew]
    in_specs = [pl.BlockSpec((seq, HEAD_DIM), blk), kv, kv, pl.BlockSpec((seq, HEAD_DIM), blk),
                pl.BlockSpec((None, 1, TAB_P), lambda h, b: (h, 0, 0)), anyspec, anyspec, anyspec, anyspec]
    aliases = {}
    if has_prev:
        aliases = {len(args): 1, len(args) + 1: 2}
        args += [prev_k, prev_v]
        in_specs += [anyspec, anyspec]
    ncp = _n_cache_copies(cache_k.shape[1], layer, bulk_layers)
    return pl.pallas_call(
        functools.partial(_attn_kernel, seq=seq, layer=layer, bulk_layers=bulk_layers, has_prev=has_prev),
        grid=(N_HEADS, batch),
        in_specs=in_specs,
        out_specs=[pl.BlockSpec((seq, HEAD_DIM), blk), anyspec, anyspec],
        out_shape=[jax.ShapeDtypeStruct((m, ATT_W), BF16),
                   jax.ShapeDtypeStruct(cache_k.shape, cache_k.dtype),
                   jax.ShapeDtypeStruct(cache_v.shape, cache_v.dtype)],
        scratch_shapes=[pltpu.VMEM((seq, HEAD_DIM), BF16), pltpu.VMEM((seq, HEAD_DIM), BF16),
                        pltpu.VMEM((QBLK, seq), F32), pltpu.VMEM((QBLK, TAB_P), F32),
                        pltpu.SemaphoreType.DMA((ncp,))],
        input_output_aliases=aliases,
        compiler_params=_cparams(("arbitrary", "arbitrary")),
        name="attn_prompt",
    )(*args)


CONV_TB = 256
CONV_HIST = 32
CONV_RC = 32
NCH = CONV_C // LANE


def _conv_kernel(ga_ref, gb_ref, gate_ref, w_ref, cb_ref, lg_ref, lb_ref, yb_ref, st_ref, ucat_ref, y_ref):
    t = pl.program_id(1)
    tb = CONV_TB

    @pl.when(t == 0)
    def _():
        ucat_ref[:, 0:CONV_HIST, :] = jnp.zeros((NCH, CONV_HIST, LANE), F32)

    @pl.when(t > 0)
    def _():
        ucat_ref[:, 0:CONV_HIST, :] = ucat_ref[:, tb:tb + CONV_HIST, :]

    for c in range(NCH):
        sl = slice(c * LANE, (c + 1) * LANE)
        ga = ga_ref[:, sl]
        ucat_ref[c, CONV_HIST:CONV_HIST + tb, :] = ga * jax.nn.sigmoid(gb_ref[:, sl])

    off = CONV_HIST - (CONV_K - 1)

    def chunk(c, carry):
        wc = w_ref[c]
        bias = cb_ref[c]
        for r in range(0, tb, CONV_RC):
            acc = jnp.broadcast_to(bias, (CONV_RC, LANE))
            for k in range(CONV_K):
                acc = acc + wc[k:k + 1, :] * ucat_ref[c, r + off + k:r + off + k + CONV_RC, :]
            y_ref[c, r:r + CONV_RC, :] = acc
        return carry

    lax.fori_loop(0, NCH, chunk, 0)

    ssum = jnp.zeros((tb, LANE), F32)
    for c in range(NCH):
        ssum = ssum + y_ref[c]
    mu = jnp.sum(ssum, axis=-1, keepdims=True) * (1.0 / CONV_C)
    sq = jnp.zeros((tb, LANE), F32)
    for c in range(NCH):
        d = y_ref[c] - mu
        sq = sq + d * d
    rstd = lax.rsqrt(jnp.sum(sq, axis=-1, keepdims=True) * (1.0 / CONV_C) + EPS)
    for c in range(NCH):
        sl = slice(c * LANE, (c + 1) * LANE)
        z = (y_ref[c] - mu) * rstd * lg_ref[c] + lb_ref[c]
        g = gate_ref[:, sl]
        yb_ref[:, sl] = (z * jax.nn.sigmoid(z) * (g * jax.nn.sigmoid(g))).astype(yb_ref.dtype)

    @pl.when(t == pl.num_programs(1) - 1)
    def _():
        for c in range(NCH):
            st_ref[:, c * LANE:(c + 1) * LANE] = ucat_ref[c, CONV_HIST + tb - (CONV_K - 1):CONV_HIST + tb, :]


def _chunked(v):
    return v.reshape(v.shape[0], NCH, LANE).transpose(1, 0, 2)


def _conv_prompt(u4, conv_w, conv_b, ln_g, ln_b, batch, seq):
    m = u4.shape[0]
    nt = seq // CONV_TB
    wpad = jnp.zeros((CONV_HIST, CONV_C), F32).at[:CONV_K].set(conv_w)
    col = lambda j: pl.BlockSpec((CONV_TB, CONV_C), lambda b, t, j=j: (b * nt + t, j))
    vec = pl.BlockSpec((NCH, 1, LANE), lambda b, t: (0, 0, 0))
    return pl.pallas_call(
        _conv_kernel,
        grid=(batch, nt),
        in_specs=[col(1), col(2), col(3), pl.BlockSpec((NCH, CONV_HIST, LANE), lambda b, t: (0, 0, 0)), vec, vec, vec],
        out_specs=[pl.BlockSpec((CONV_TB, CONV_C), lambda b, t: (b * nt + t, 0)),
                   pl.BlockSpec((None, CONV_K - 1, CONV_C), lambda b, t: (b, 0, 0))],
        out_shape=[jax.ShapeDtypeStruct((m, CONV_C), BF16),
                   jax.ShapeDtypeStruct((batch, CONV_K - 1, CONV_C), F32)],
        scratch_shapes=[pltpu.VMEM((NCH, CONV_HIST + CONV_TB, LANE), F32), pltpu.VMEM((NCH, CONV_TB, LANE), F32)],
        compiler_params=_cparams(("parallel", "arbitrary")),
        name="conv_prompt",
    )(u4, u4, u4, _chunked(wpad), _chunked(conv_b.reshape(1, -1)), _chunked(ln_g.reshape(1, -1)),
      _chunked(ln_b.reshape(1, -1)))


def _conv_s_kernel(u4_ref, st_ref, w_ref, cb_ref, lg_ref, lb_ref, yb_ref, so_ref, *, nb):
    ga = u4_ref[:, CONV_C:2 * CONV_C]
    u = ga * jax.nn.sigmoid(u4_ref[:, 2 * CONV_C:3 * CONV_C])
    gate = u4_ref[:, 3 * CONV_C:4 * CONV_C]
    yb_ref[...] = jnp.zeros(yb_ref.shape, yb_ref.dtype)
    for b in range(nb):
        st = st_ref[b]
        un = u[b:b + 1, :]
        y = (jnp.sum(st * w_ref[0:CONV_K - 1, :], axis=0, keepdims=True)
             + un * w_ref[CONV_K - 1:CONV_K, :] + cb_ref[...])
        mu = jnp.mean(y, axis=-1, keepdims=True)
        d = y - mu
        z = d * lax.rsqrt(jnp.mean(d * d, axis=-1, keepdims=True) + EPS) * lg_ref[...] + lb_ref[...]
        g = gate[b:b + 1, :]
        yb_ref[b:b + 1, :] = (z * jax.nn.sigmoid(z) * (g * jax.nn.sigmoid(g))).astype(yb_ref.dtype)
        so_ref[b, 0:CONV_K - 2, :] = st[1:CONV_K - 1, :]
        so_ref[b, CONV_K - 2:CONV_K - 1, :] = un


def _conv_sample(u4s, state, conv_w, conv_b, ln_g, ln_b):
    nb = state.shape[0]
    return pl.pallas_call(
        functools.partial(_conv_s_kernel, nb=nb),
        out_shape=[jax.ShapeDtypeStruct((SAMPLE_ROWS, CONV_C), F32),
                   jax.ShapeDtypeStruct(state.shape, F32)],
        compiler_params=pltpu.CompilerParams(vmem_limit_bytes=VMEM_LIMIT),
        name="conv_sample",
    )(u4s, state, conv_w, conv_b.reshape(1, -1), ln_g.reshape(1, -1), ln_b.reshape(1, -1))


def _attn_s_kernel(q_ref, kn_ref, vn_ref, g_ref, k1_ref, k2_ref, k3_ref, v1_ref, v2_ref, v3_ref,
                   tab_ref, tnew_ref, o_ref):
    q = q_ref[...]
    s = [jnp.sum(kr[...] * q[None], axis=-1, keepdims=True) + tab_ref[p]
         for p, kr in enumerate((k1_ref, k2_ref, k3_ref))]
    s_new = jnp.sum(q * kn_ref[...], axis=-1, keepdims=True) + tnew_ref[...]
    m = s_new
    for sp in s:
        m = jnp.maximum(m, jnp.max(sp, axis=0))
    p_new = jnp.exp(s_new - m)
    l = p_new
    acc = p_new * vn_ref[...]
    for sp, vr in zip(s, (v1_ref, v2_ref, v3_ref)):
        p = jnp.exp(sp - m[None])
        l = l + jnp.sum(p, axis=0)
        acc = acc + jnp.sum(p * vr[...], axis=0)
    g = g_ref[...]
    o_ref[...] = acc * (1.0 / l) * (g * jax.nn.sigmoid(g))


def _attn_sample(qs, kn, vn, gs, cache_k, cache_v, tab_s, tab_new, layer):
    nb = qs.shape[0]
    rows = 128
    row = pl.BlockSpec((None, N_HEADS, HEAD_DIM), lambda b: (b, 0, 0))
    c1 = lambda c: c.reshape(DEPTH, nb, WIN_MAX // rows, rows, N_HEADS, HEAD_DIM)
    c2 = lambda c: c.reshape(DEPTH, nb, WIN_MAX // 4, 4, N_HEADS, HEAD_DIM)
    c3 = lambda c: c.reshape(DEPTH, nb, WIN_MAX // 16, 16, N_HEADS, HEAD_DIM)
    s1 = pl.BlockSpec((None, None, None, rows, N_HEADS, HEAD_DIM), lambda b: (layer, b, WIN_MAX // rows - 1, 0, 0, 0))
    s2 = pl.BlockSpec((None, None, rows, None, N_HEADS, HEAD_DIM), lambda b: (layer, b, 3, 0, 0, 0))
    s3 = pl.BlockSpec((None, None, rows, None, N_HEADS, HEAD_DIM), lambda b: (layer, b, 0, 0, 0, 0))
    return pl.pallas_call(
        _attn_s_kernel,
        grid=(nb,),
        in_specs=[row, row, row, row, s1, s2, s3, s1, s2, s3,
                  pl.BlockSpec((3, rows, N_HEADS, 1), lambda b: (0, 0, 0, 0)),
                  pl.BlockSpec((N_HEADS, 1), lambda b: (0, 0))],
        out_specs=row,
        out_shape=jax.ShapeDtypeStruct((nb, N_HEADS, HEAD_DIM), F32),
        compiler_params=_cparams(("parallel",)),
        name="attn_sample",
    )(qs, kn, vn, gs, c1(cache_k), c2(cache_k), c3(cache_k), c1(cache_v), c2(cache_v), c3(cache_v),
      tab_s, tab_new)


def _rel_bucket(dist):
    max_exact = N_BUCKETS // 2
    df = jnp.maximum(dist, 1).astype(F32)
    large = max_exact + (jnp.log(df / max_exact) / math.log(MAX_DISTANCE / max_exact)
                         * (N_BUCKETS - max_exact)).astype(jnp.int32)
    large = jnp.minimum(large, N_BUCKETS - 1)
    return jnp.where(dist < max_exact, dist, large)


def _bias_tables(rel_bias):
    dist = jnp.arange(WIN_MAX + 1, dtype=jnp.int32)
    bias = rel_bias[_rel_bucket(dist)].astype(F32).T
    mult = ((dist <= 128).astype(F32) + ((dist % 4 == 0) & (dist <= 512)).astype(F32)
            + (dist % 16 == 0).astype(F32))
    t1d = jnp.where(mult > 0, bias + jnp.log(jnp.maximum(mult, 1.0)), NEG)
    trow = jnp.full((N_HEADS, 1, TAB_P), NEG, F32).at[:, 0, :WIN_MAX + 1].set(t1d[:, ::-1])
    j = jnp.arange(128, dtype=jnp.int32)
    d_rows = jnp.stack([128 - j, 512 - 4 * j, 2048 - 16 * j])
    tab_s = bias[:, d_rows].transpose(1, 2, 0)[..., None]
    tab_new = (bias[:, 0] + math.log(3.0)).reshape(N_HEADS, 1)
    return trow, tab_s, tab_new


def kernel(x_prompt, x_sample, cache_k, cache_v, state_conv, p_prompt, p_sample, rel_bias, g_pre, w_in, conv_w,
           conv_b, ln_g, ln_b, w_out, g_post, w_ple, g_ple, w_pg, b_pg):
    batch, seq, d = x_prompt.shape
    nb = x_sample.shape[0]
    m = batch * seq
    pad = lambda a: jnp.zeros((SAMPLE_ROWS, a.size // nb), a.dtype).at[:nb].set(a.reshape(nb, -1))
    heads = lambda a: a[:nb, :ATT_W].reshape(nb, N_HEADS, HEAD_DIM)

    hp = x_prompt.reshape(m, d)
    hs = pad(x_sample)
    trow, tab_s, tab_new = _bias_tables(rel_bias)
    kst = vst = new_k = new_v = None
    cp_l, cs_l = [], []
    for l in range(DEPTH):
        xn = _rmsnorm(hp, g_pre[l], 256)
        xns = _rmsnorm(hs, g_pre[l], SAMPLE_ROWS)
        q, qs = _mm([xn], [xns], w_in[l], 0, ATT_W, scale=HEAD_DIM ** -0.5, out_dtype=BF16, name="proj_q")
        kst, ks = _mm([xn], [xns], w_in[l], ATT_W, ATT_W, name="proj_k", prev=kst)
        vst, vs = _mm([xn], [xns], w_in[l], 2 * ATT_W, ATT_W, name="proj_v", prev=vst)
        u4, u4s = _mm([xn], [xns], w_in[l], 3 * ATT_W, ATT_W + 3 * CONV_C, name="proj_rest")

        kn, vn = heads(ks), heads(vs)
        ya, new_k, new_v = _attn_prompt(q, kst, vst, u4, trow, cache_k, cache_v, kn, vn, new_k, new_v,
                                        l, batch, seq)
        yb, cstate = _conv_prompt(u4, conv_w[l], conv_b[l], ln_g[l], ln_b[l], batch, seq)

        yas = _attn_sample(heads(qs.astype(F32)), kn, vn, heads(u4s), cache_k, cache_v, tab_s, tab_new, l)
        ybs, cstate_s = _conv_sample(u4s, state_conv[l], conv_w[l], conv_b[l], ln_g[l], ln_b[l])

        y, ys = _mm([ya, yb], [pad(yas).astype(BF16), ybs.astype(BF16)], w_out[l], 0, d, name="proj_out")
        wple_bf = w_ple[l].astype(BF16)
        h1, h1b, pen = _post(hp, y, p_prompt[l].reshape(m, PLE_DIM), g_post[l], wple_bf, g_ple[l], 128)
        h1s, h1bs, pens = _post(hs, ys, pad(p_sample[l]), g_post[l], wple_bf, g_ple[l], SAMPLE_ROWS)
        hp, hs = _gate_mm(h1b, h1bs, w_pg[l], b_pg[l], h1, h1s, pen, pens)
        cp_l.append(cstate)
        cs_l.append(cstate_s)

    kv_shape = (DEPTH, batch, seq, N_HEADS, HEAD_DIM)
    return (hp.reshape(batch, seq, d), hs[:nb].reshape(nb, 1, d),
            kst.reshape(kv_shape), vst.reshape(kv_shape), jnp.stack(cp_l),
            new_k, new_v, jnp.stack(cs_l))
```

```python
import functools
import math

import jax
import jax.numpy as jnp
from jax import lax
from jax.experimental import pallas as pl
from jax.experimental.pallas import tpu as pltpu

D_MODEL = 4096
HEAD_DIM = 128
ATT_W = 2048
N_HEADS = 16
CONV_C = 2048
CONV_K = 31
PLE_DIM = 256
N_BUCKETS = 32
MAX_DISTANCE = 2048
WIN_MAX = 2048
EPS = 1e-6
NEG = -1e30
DEPTH = 2

F32 = jnp.float32
BF16 = jnp.bfloat16

VMEM_LIMIT = 56 * 1024 * 1024
SAMPLE_ROWS = 16
LANE = 128
QBLK = 128
ATT_CHUNK = 256
TAB_W = WIN_MAX + QBLK
TAB_P = TAB_W + QBLK


def _cparams(sem):
    return pltpu.CompilerParams(dimension_semantics=sem, vmem_limit_bytes=VMEM_LIMIT)


def _rmsnorm_kernel(x_ref, g_ref, o_ref):
    x = x_ref[...]
    r = x * lax.rsqrt(jnp.mean(x * x, axis=-1, keepdims=True) + EPS)
    o_ref[...] = (r * g_ref[...]).astype(o_ref.dtype)


def _rmsnorm(x, g, tm):
    m, d = x.shape
    return pl.pallas_call(
        _rmsnorm_kernel,
        grid=(m // tm,),
        in_specs=[pl.BlockSpec((tm, d), lambda i: (i, 0)), pl.BlockSpec((1, d), lambda i: (0, 0))],
        out_specs=pl.BlockSpec((tm, d), lambda i: (i, 0)),
        out_shape=jax.ShapeDtypeStruct((m, d), BF16),
        compiler_params=_cparams(("parallel",)),
        name="rmsnorm",
    )(x, g.reshape(1, d))


def _mm_kernel(*refs, nparts, scale, has_prev):
    xs = refs[:nparts]
    ss = refs[nparts:2 * nparts]
    ws = refs[2 * nparts:3 * nparts]
    no = 3 * nparts + (1 if has_prev else 0)
    o_ref, os_ref = refs[no:no + 2]
    wbs = refs[no + 2:]

    @pl.when(pl.program_id(1) == 0)
    def _():
        acc = None
        for s_ref, w_ref, wb_ref in zip(ss, ws, wbs):
            wb_ref[...] = w_ref[...].astype(BF16)
            d = jnp.dot(s_ref[...], wb_ref[...], preferred_element_type=F32)
            acc = d if acc is None else acc + d
        os_ref[...] = (acc * scale).astype(os_ref.dtype)

    acc = None
    for x_ref, wb_ref in zip(xs, wbs):
        d = jnp.dot(x_ref[...], wb_ref[...], preferred_element_type=F32)
        acc = d if acc is None else acc + d
    res = (acc * scale).astype(o_ref.dtype)
    if has_prev:
        o_ref[0] = refs[3 * nparts][...]
        o_ref[1] = res
    else:
        o_ref[...] = res


def _mm(x_parts, s_parts, w, layer, col0, ncols, *, scale=1.0, out_dtype=F32, tm=1024, tn=512, name="mm", prev=None):
    nparts = len(x_parts)
    m = x_parts[0].shape[0]
    ms = s_parts[0].shape[0]
    kparts = [x.shape[1] for x in x_parts]
    assert sum(kparts) == w.shape[1] and col0 % tn == 0 and ncols % tn == 0 and m % tm == 0
    jb0 = col0 // tn
    koff = [sum(kparts[:p]) for p in range(nparts)]
    in_specs = (
        [pl.BlockSpec((tm, kp), lambda j, i: (i, 0)) for kp in kparts]
        + [pl.BlockSpec((ms, kp), lambda j, i: (0, 0)) for kp in kparts]
        + [pl.BlockSpec((None, kp, tn), functools.partial(lambda j, i, kb: (layer, kb, jb0 + j), kb=ko // kp))
           for kp, ko in zip(kparts, koff)]
    )
    args = [*x_parts, *s_parts, *([w] * nparts)]
    if prev is None:
        big_spec = pl.BlockSpec((tm, tn), lambda j, i: (i, j))
        big_shape = jax.ShapeDtypeStruct((m, ncols), out_dtype)
    else:
        assert DEPTH == 2 and prev.shape == (m, ncols) and prev.dtype == out_dtype
        in_specs.append(pl.BlockSpec((tm, tn), lambda j, i: (i, j)))
        args.append(prev)
        big_spec = pl.BlockSpec((DEPTH, tm, tn), lambda j, i: (0, i, j))
        big_shape = jax.ShapeDtypeStruct((DEPTH, m, ncols), out_dtype)
    return pl.pallas_call(
        functools.partial(_mm_kernel, nparts=nparts, scale=scale, has_prev=prev is not None),
        grid=(ncols // tn, m // tm),
        in_specs=in_specs,
        out_specs=[big_spec, pl.BlockSpec((ms, tn), lambda j, i: (0, j))],
        out_shape=[big_shape, jax.ShapeDtypeStruct((ms, ncols), out_dtype)],
        scratch_shapes=[pltpu.VMEM((kp, tn), BF16) for kp in kparts],
        compiler_params=_cparams(("parallel", "arbitrary")),
        name=name,
    )(*args)


def _gate_update(acc, b_ref, h_ref, p_ref, wple_ref, prs_ref, gple_ref):
    pe = jnp.dot(p_ref[...].astype(BF16), wple_ref[...], preferred_element_type=F32)
    pen = pe * prs_ref[:, 0:1] * gple_ref[...]
    return h_ref[...] + jax.nn.sigmoid(acc + b_ref[...]) * pen


def _gate_kernel(x_ref, s_ref, w_ref, b_ref, h_ref, hs_ref, p_ref, ps_ref, wple_ref, prs_ref, prss_ref, gple_ref,
                 o_ref, os_ref, wb_ref):
    @pl.when(pl.program_id(1) == 0)
    def _():
        wb_ref[...] = w_ref[...].astype(BF16)
        acc = jnp.dot(s_ref[...], wb_ref[...], preferred_element_type=F32)
        os_ref[...] = _gate_update(acc, b_ref, hs_ref, ps_ref, wple_ref, prss_ref, gple_ref)

    acc = jnp.dot(x_ref[...], wb_ref[...], preferred_element_type=F32)
    o_ref[...] = _gate_update(acc, b_ref, h_ref, p_ref, wple_ref, prs_ref, gple_ref)


def _gate_mm(x, s, w, b, layer, h, hs, p, ps, w_ple_bf, prs, prss, g_ple, *, tm=1024, tn=512):
    m, k = x.shape
    ms = s.shape[0]
    n = w.shape[2]
    big = pl.BlockSpec((tm, tn), lambda j, i: (i, j))
    small = pl.BlockSpec((ms, tn), lambda j, i: (0, j))
    return pl.pallas_call(
        _gate_kernel,
        grid=(n // tn, m // tm),
        in_specs=[pl.BlockSpec((tm, k), lambda j, i: (i, 0)), pl.BlockSpec((ms, k), lambda j, i: (0, 0)),
                  pl.BlockSpec((None, k, tn), lambda j, i: (layer, 0, j)),
                  pl.BlockSpec((None, 1, tn), lambda j, i: (layer, 0, j)),
                  big, small,
                  pl.BlockSpec((None, tm, PLE_DIM), lambda j, i: (layer, i, 0)),
                  pl.BlockSpec((None, ms, PLE_DIM), lambda j, i: (layer, 0, 0)),
                  pl.BlockSpec((PLE_DIM, tn), lambda j, i: (0, j)),
                  pl.BlockSpec((tm, LANE), lambda j, i: (i, 0)), pl.BlockSpec((ms, LANE), lambda j, i: (0, 0)),
                  pl.BlockSpec((1, tn), lambda j, i: (0, j))],
        out_specs=[big, small],
        out_shape=[jax.ShapeDtypeStruct((m, n), F32), jax.ShapeDtypeStruct((ms, n), F32)],
        scratch_shapes=[pltpu.VMEM((k, tn), BF16)],
        compiler_params=_cparams(("parallel", "arbitrary")),
        name="gate_mm",
    )(x, s, w, b.reshape(DEPTH, 1, n), h, hs, p, ps, w_ple_bf, prs, prss, g_ple.reshape(1, n))


def _post_kernel(h_ref, y_ref, p_ref, gpost_ref, wple_ref, h1_ref, h1b_ref, prs_ref):
    y = y_ref[...]
    r = y * lax.rsqrt(jnp.mean(y * y, axis=-1, keepdims=True) + EPS) * gpost_ref[...]
    h1 = h_ref[...] + r
    h1_ref[...] = h1
    h1b_ref[...] = h1.astype(BF16)
    pe = jnp.dot(p_ref[...].astype(BF16), wple_ref[...], preferred_element_type=F32)
    rs = lax.rsqrt(jnp.mean(pe * pe, axis=-1, keepdims=True) + EPS)
    prs_ref[...] = jnp.broadcast_to(rs, prs_ref.shape)


def _post(h, y, p, layer, g_post, w_ple_bf, tm):
    m, d = h.shape
    row = pl.BlockSpec((tm, d), lambda i: (i, 0))
    vec = pl.BlockSpec((1, d), lambda i: (0, 0))
    return pl.pallas_call(
        _post_kernel,
        grid=(m // tm,),
        in_specs=[row, row, pl.BlockSpec((None, tm, PLE_DIM), lambda i: (layer, i, 0)), vec,
                  pl.BlockSpec((PLE_DIM, d), lambda i: (0, 0))],
        out_specs=[row, row, pl.BlockSpec((tm, LANE), lambda i: (i, 0))],
        out_shape=[jax.ShapeDtypeStruct((m, d), F32), jax.ShapeDtypeStruct((m, d), BF16),
                   jax.ShapeDtypeStruct((m, LANE), F32)],
        compiler_params=_cparams(("parallel",)),
        name="post_norm",
    )(h, y, p, g_post.reshape(1, d), w_ple_bf)


def _shift_window(c_ref, nxt_ref, new_ref, o_ref, is_last):
    t = c_ref.shape[0]
    o_ref[0:t - 1] = c_ref[1:t]
    o_ref[t - 1] = jnp.where(is_last, new_ref[...], nxt_ref[0])


def _attn_kernel(*refs, seq, cache_chunks):
    q_ref, k_ref, v_ref, g_ref, trow_ref = refs[:5]
    hid, bid = pl.program_id(0), pl.program_id(1)
    if cache_chunks:
        ck_ref, ckn_ref, kn_ref, cv_ref, cvn_ref, vn_ref, o_ref, ok_ref, ov_ref = refs[5:14]
        kb_ref, vb_ref, s_ref, tab_ref = refs[14:]
        is_last = (hid * pl.num_programs(1) + bid) % cache_chunks == cache_chunks - 1
        _shift_window(ck_ref, ckn_ref, kn_ref, ok_ref, is_last)
        _shift_window(cv_ref, cvn_ref, vn_ref, ov_ref, is_last)
    else:
        o_ref, kb_ref, vb_ref, s_ref, tab_ref = refs[5:]

    @pl.when(bid == 0)
    def _():
        tab_ref[...] = pltpu.roll(jnp.broadcast_to(trow_ref[...], (QBLK, TAB_P)), 0, 1, stride=1, stride_axis=0)

    kb_ref[...] = k_ref[...].astype(BF16)
    vb_ref[...] = v_ref[...].astype(BF16)
    nblk = seq // QBLK
    for i in range(nblk):
        q = q_ref[i * QBLK:(i + 1) * QBLK, :]
        nkeys = (i + 1) * QBLK
        tab0 = (nblk - i) * QBLK
        chunks = [(c, min(ATT_CHUNK, nkeys - c)) for c in range(0, nkeys, ATT_CHUNK)]
        mx = jnp.full((QBLK, LANE), NEG, F32)
        for c, w in chunks:
            s = lax.dot_general(q, kb_ref[c:c + w, :], (((1,), (1,)), ((), ())), preferred_element_type=F32)
            s = s + tab_ref[:, tab0 + c:tab0 + c + w]
            s_ref[:, c:c + w] = s
            for t in range(0, w, LANE):
                mx = jnp.maximum(mx, s[:, t:t + LANE])
        m = jnp.max(mx, axis=-1, keepdims=True)
        ls = jnp.zeros((QBLK, LANE), F32)
        acc = jnp.zeros((QBLK, HEAD_DIM), F32)
        for c, w in chunks:
            p = jnp.exp(s_ref[:, c:c + w] - m)
            for t in range(0, w, LANE):
                ls = ls + p[:, t:t + LANE]
            acc = acc + jnp.dot(p.astype(BF16), vb_ref[c:c + w, :], preferred_element_type=F32)
        l = jnp.sum(ls, axis=-1, keepdims=True)
        g = g_ref[i * QBLK:(i + 1) * QBLK, :]
        o = acc * (1.0 / l)
        o_ref[i * QBLK:(i + 1) * QBLK, :] = (o * (g * jax.nn.sigmoid(g))).astype(o_ref.dtype)


def _attn_prompt(q, kst, vst, u4, trow, layer, batch, seq, caches=None):
    m = q.shape[0]
    blk = lambda h, b: (b, h)
    if kst.ndim == 3:
        kv = pl.BlockSpec((None, seq, HEAD_DIM), lambda h, b: (layer, b, h))
    else:
        kv = pl.BlockSpec((seq, HEAD_DIM), blk)
    args = [q, kst, vst, u4, trow]
    in_specs = [pl.BlockSpec((seq, HEAD_DIM), blk), kv, kv, pl.BlockSpec((seq, HEAD_DIM), blk),
                pl.BlockSpec((None, 1, TAB_P), lambda h, b: (h, 0, 0))]
    out_specs = [pl.BlockSpec((seq, HEAD_DIM), blk)]
    out_shape = [jax.ShapeDtypeStruct((m, ATT_W), BF16)]
    cpw = 0
    if caches is not None:
        cache_k, cache_v, knew, vnew = caches
        depth, nb, win = cache_k.shape[:3]
        steps = N_HEADS * batch
        assert (depth * nb * win) % steps == 0
        rows = depth * nb * win // steps
        assert win % rows == 0
        cpw = win // rows
        where = lambda h, b: ((h * batch + b) // (nb * cpw), ((h * batch + b) // cpw) % nb, (h * batch + b) % cpw)
        chunk = pl.BlockSpec((None, None, rows, N_HEADS, HEAD_DIM), lambda h, b: (*where(h, b), 0, 0))
        nxt = pl.BlockSpec((None, None, 1, N_HEADS, HEAD_DIM),
                           lambda h, b: (*where(h, b)[:2], jnp.minimum((where(h, b)[2] + 1) * rows, win - 1), 0, 0))
        new = pl.BlockSpec((None, None, N_HEADS, HEAD_DIM), lambda h, b: (*where(h, b)[:2], 0, 0))
        args += [cache_k, cache_k, knew, cache_v, cache_v, vnew]
        in_specs += [chunk, nxt, new, chunk, nxt, new]
        out_specs += [chunk, chunk]
        out_shape += [jax.ShapeDtypeStruct(cache_k.shape, cache_k.dtype),
                      jax.ShapeDtypeStruct(cache_v.shape, cache_v.dtype)]
    return pl.pallas_call(
        functools.partial(_attn_kernel, seq=seq, cache_chunks=cpw),
        grid=(N_HEADS, batch),
        in_specs=in_specs,
        out_specs=out_specs,
        out_shape=out_shape,
        scratch_shapes=[pltpu.VMEM((seq, HEAD_DIM), BF16), pltpu.VMEM((seq, HEAD_DIM), BF16),
                        pltpu.VMEM((QBLK, seq), F32), pltpu.VMEM((QBLK, TAB_P), F32)],
        compiler_params=_cparams(("arbitrary", "arbitrary")),
        name="attn_prompt",
    )(*args)


CONV_TB = 256
CONV_HIST = 32
CONV_RC = 32
NCH = CONV_C // LANE


def _conv_kernel(ga_ref, gb_ref, gate_ref, w_ref, cb_ref, lg_ref, lb_ref, yb_ref, st_ref, ucat_ref, y_ref):
    t = pl.program_id(1)
    tb = CONV_TB

    @pl.when(t == 0)
    def _():
        ucat_ref[:, 0:CONV_HIST, :] = jnp.zeros((NCH, CONV_HIST, LANE), F32)

    @pl.when(t > 0)
    def _():
        ucat_ref[:, 0:CONV_HIST, :] = ucat_ref[:, tb:tb + CONV_HIST, :]

    for c in range(NCH):
        sl = slice(c * LANE, (c + 1) * LANE)
        ga = ga_ref[:, sl]
        ucat_ref[c, CONV_HIST:CONV_HIST + tb, :] = ga * jax.nn.sigmoid(gb_ref[:, sl])

    off = CONV_HIST - (CONV_K - 1)

    def chunk(c, carry):
        wc = w_ref[c]
        bias = cb_ref[c]
        for r in range(0, tb, CONV_RC):
            acc = jnp.broadcast_to(bias, (CONV_RC, LANE))
            for k in range(CONV_K):
                acc = acc + wc[k:k + 1, :] * ucat_ref[c, r + off + k:r + off + k + CONV_RC, :]
            y_ref[c, r:r + CONV_RC, :] = acc
        return carry

    lax.fori_loop(0, NCH, chunk, 0)

    ssum = jnp.zeros((tb, LANE), F32)
    for c in range(NCH):
        ssum = ssum + y_ref[c]
    mu = jnp.sum(ssum, axis=-1, keepdims=True) * (1.0 / CONV_C)
    sq = jnp.zeros((tb, LANE), F32)
    for c in range(NCH):
        d = y_ref[c] - mu
        sq = sq + d * d
    rstd = lax.rsqrt(jnp.sum(sq, axis=-1, keepdims=True) * (1.0 / CONV_C) + EPS)
    for c in range(NCH):
        sl = slice(c * LANE, (c + 1) * LANE)
        z = (y_ref[c] - mu) * rstd * lg_ref[c] + lb_ref[c]
        g = gate_ref[:, sl]
        yb_ref[:, sl] = (z * jax.nn.sigmoid(z) * (g * jax.nn.sigmoid(g))).astype(yb_ref.dtype)

    @pl.when(t == pl.num_programs(1) - 1)
    def _():
        for c in range(NCH):
            st_ref[:, c * LANE:(c + 1) * LANE] = ucat_ref[c, CONV_HIST + tb - (CONV_K - 1):CONV_HIST + tb, :]


def _chunked(v):
    return v.reshape(v.shape[0], NCH, LANE).transpose(1, 0, 2)


def _conv_prompt(u4, conv_w, conv_b, ln_g, ln_b, batch, seq):
    m = u4.shape[0]
    nt = seq // CONV_TB
    wpad = jnp.zeros((CONV_HIST, CONV_C), F32).at[:CONV_K].set(conv_w)
    col = lambda j: pl.BlockSpec((CONV_TB, CONV_C), lambda b, t, j=j: (b * nt + t, j))
    vec = pl.BlockSpec((NCH, 1, LANE), lambda b, t: (0, 0, 0))
    return pl.pallas_call(
        _conv_kernel,
        grid=(batch, nt),
        in_specs=[col(1), col(2), col(3), pl.BlockSpec((NCH, CONV_HIST, LANE), lambda b, t: (0, 0, 0)), vec, vec, vec],
        out_specs=[pl.BlockSpec((CONV_TB, CONV_C), lambda b, t: (b * nt + t, 0)),
                   pl.BlockSpec((None, CONV_K - 1, CONV_C), lambda b, t: (b, 0, 0))],
        out_shape=[jax.ShapeDtypeStruct((m, CONV_C), BF16),
                   jax.ShapeDtypeStruct((batch, CONV_K - 1, CONV_C), F32)],
        scratch_shapes=[pltpu.VMEM((NCH, CONV_HIST + CONV_TB, LANE), F32), pltpu.VMEM((NCH, CONV_TB, LANE), F32)],
        compiler_params=_cparams(("parallel", "arbitrary")),
        name="conv_prompt",
    )(u4, u4, u4, _chunked(wpad), _chunked(conv_b.reshape(1, -1)), _chunked(ln_g.reshape(1, -1)),
      _chunked(ln_b.reshape(1, -1)))


def _conv_s_kernel(u4_ref, st_ref, w_ref, cb_ref, lg_ref, lb_ref, yb_ref, so_ref, *, nb):
    ga = u4_ref[:, CONV_C:2 * CONV_C]
    u = ga * jax.nn.sigmoid(u4_ref[:, 2 * CONV_C:3 * CONV_C])
    gate = u4_ref[:, 3 * CONV_C:4 * CONV_C]
    yb_ref[...] = jnp.zeros(yb_ref.shape, yb_ref.dtype)
    for b in range(nb):
        st = st_ref[b]
        un = u[b:b + 1, :]
        y = (jnp.sum(st * w_ref[0:CONV_K - 1, :], axis=0, keepdims=True)
             + un * w_ref[CONV_K - 1:CONV_K, :] + cb_ref[...])
        mu = jnp.mean(y, axis=-1, keepdims=True)
        d = y - mu
        z = d * lax.rsqrt(jnp.mean(d * d, axis=-1, keepdims=True) + EPS) * lg_ref[...] + lb_ref[...]
        g = gate[b:b + 1, :]
        yb_ref[b:b + 1, :] = (z * jax.nn.sigmoid(z) * (g * jax.nn.sigmoid(g))).astype(yb_ref.dtype)
        so_ref[b, 0:CONV_K - 2, :] = st[1:CONV_K - 1, :]
        so_ref[b, CONV_K - 2:CONV_K - 1, :] = un


def _conv_sample(u4s, state, conv_w, conv_b, ln_g, ln_b):
    nb = state.shape[0]
    return pl.pallas_call(
        functools.partial(_conv_s_kernel, nb=nb),
        out_shape=[jax.ShapeDtypeStruct((SAMPLE_ROWS, CONV_C), F32),
                   jax.ShapeDtypeStruct(state.shape, F32)],
        compiler_params=pltpu.CompilerParams(vmem_limit_bytes=VMEM_LIMIT),
        name="conv_sample",
    )(u4s, state, conv_w, conv_b.reshape(1, -1), ln_g.reshape(1, -1), ln_b.reshape(1, -1))


def _attn_s_kernel(q_ref, kn_ref, vn_ref, g_ref, k1_ref, k2_ref, k3_ref, v1_ref, v2_ref, v3_ref,
                   tab_ref, tnew_ref, o_ref):
    q = q_ref[...]
    s = [jnp.sum(kr[...] * q[None], axis=-1, keepdims=True) + tab_ref[p]
         for p, kr in enumerate((k1_ref, k2_ref, k3_ref))]
    s_new = jnp.sum(q * kn_ref[...], axis=-1, keepdims=True) + tnew_ref[...]
    m = s_new
    for sp in s:
        m = jnp.maximum(m, jnp.max(sp, axis=0))
    p_new = jnp.exp(s_new - m)
    l = p_new
    acc = p_new * vn_ref[...]
    for sp, vr in zip(s, (v1_ref, v2_ref, v3_ref)):
        p = jnp.exp(sp - m[None])
        l = l + jnp.sum(p, axis=0)
        acc = acc + jnp.sum(p * vr[...], axis=0)
    g = g_ref[...]
    o_ref[...] = acc * (1.0 / l) * (g * jax.nn.sigmoid(g))


def _attn_sample(qs, kn, vn, gs, cache_k, cache_v, tab_s, tab_new, layer):
    nb = qs.shape[0]
    rows = 128
    row = pl.BlockSpec((None, N_HEADS, HEAD_DIM), lambda b: (b, 0, 0))
    c1 = lambda c: c.reshape(DEPTH, nb, WIN_MAX // rows, rows, N_HEADS, HEAD_DIM)
    c2 = lambda c: c.reshape(DEPTH, nb, WIN_MAX // 4, 4, N_HEADS, HEAD_DIM)
    c3 = lambda c: c.reshape(DEPTH, nb, WIN_MAX // 16, 16, N_HEADS, HEAD_DIM)
    s1 = pl.BlockSpec((None, None, None, rows, N_HEADS, HEAD_DIM), lambda b: (layer, b, WIN_MAX // rows - 1, 0, 0, 0))
    s2 = pl.BlockSpec((None, None, rows, None, N_HEADS, HEAD_DIM), lambda b: (layer, b, 3, 0, 0, 0))
    s3 = pl.BlockSpec((None, None, rows, None, N_HEADS, HEAD_DIM), lambda b: (layer, b, 0, 0, 0, 0))
    return pl.pallas_call(
        _attn_s_kernel,
        grid=(nb,),
        in_specs=[row, row, row, row, s1, s2, s3, s1, s2, s3,
                  pl.BlockSpec((3, rows, N_HEADS, 1), lambda b: (0, 0, 0, 0)),
                  pl.BlockSpec((N_HEADS, 1), lambda b: (0, 0))],
        out_specs=row,
        out_shape=jax.ShapeDtypeStruct((nb, N_HEADS, HEAD_DIM), F32),
        compiler_params=_cparams(("parallel",)),
        name="attn_sample",
    )(qs, kn, vn, gs, c1(cache_k), c2(cache_k), c3(cache_k), c1(cache_v), c2(cache_v), c3(cache_v),
      tab_s, tab_new)


def _rel_bucket(dist):
    max_exact = N_BUCKETS // 2
    df = jnp.maximum(dist, 1).astype(F32)
    large = max_exact + (jnp.log(df / max_exact) / math.log(MAX_DISTANCE / max_exact)
                         * (N_BUCKETS - max_exact)).astype(jnp.int32)
    large = jnp.minimum(large, N_BUCKETS - 1)
    return jnp.where(dist < max_exact, dist, large)


def _bias_tables(rel_bias):
    dist = jnp.arange(WIN_MAX + 1, dtype=jnp.int32)
    bias = rel_bias[_rel_bucket(dist)].astype(F32).T
    mult = ((dist <= 128).astype(F32) + ((dist % 4 == 0) & (dist <= 512)).astype(F32)
            + (dist % 16 == 0).astype(F32))
    t1d = jnp.where(mult > 0, bias + jnp.log(jnp.maximum(mult, 1.0)), NEG)
    trow = jnp.full((N_HEADS, 1, TAB_P), NEG, F32).at[:, 0, :WIN_MAX + 1].set(t1d[:, ::-1])
    j = jnp.arange(128, dtype=jnp.int32)
    d_rows = jnp.stack([128 - j, 512 - 4 * j, 2048 - 16 * j])
    tab_s = bias[:, d_rows].transpose(1, 2, 0)[..., None]
    tab_new = (bias[:, 0] + math.log(3.0)).reshape(N_HEADS, 1)
    return trow, tab_s, tab_new


def kernel(x_prompt, x_sample, cache_k, cache_v, state_conv, p_prompt, p_sample, rel_bias, g_pre, w_in, conv_w,
           conv_b, ln_g, ln_b, w_out, g_post, w_ple, g_ple, w_pg, b_pg):
    batch, seq, d = x_prompt.shape
    nb = x_sample.shape[0]
    m = batch * seq
    pad = lambda a: jnp.zeros((SAMPLE_ROWS, a.size // nb), a.dtype).at[:nb].set(a.reshape(nb, -1))
    heads = lambda a: a[:nb, :ATT_W].reshape(nb, N_HEADS, HEAD_DIM)

    hp = x_prompt.reshape(m, d)
    hs = pad(x_sample)
    trow, tab_s, tab_new = _bias_tables(rel_bias)
    p_all = p_prompt.reshape(DEPTH, m, PLE_DIM)
    ps_all = jnp.stack([pad(p_sample[l]) for l in range(DEPTH)])
    wple_bf = w_ple.astype(BF16)
    kst = vst = new_k = new_v = None
    cp_l, cs_l, kn_l, vn_l = [], [], [], []
    for l in range(DEPTH):
        xn = _rmsnorm(hp, g_pre[l], 256)
        xns = _rmsnorm(hs, g_pre[l], SAMPLE_ROWS)
        q, qs = _mm([xn], [xns], w_in, l, 0, ATT_W, scale=HEAD_DIM ** -0.5, out_dtype=BF16, name="proj_q")
        kst, ks = _mm([xn], [xns], w_in, l, ATT_W, ATT_W, name="proj_k", prev=kst)
        vst, vs = _mm([xn], [xns], w_in, l, 2 * ATT_W, ATT_W, name="proj_v", prev=vst)
        u4, u4s = _mm([xn], [xns], w_in, l, 3 * ATT_W, ATT_W + 3 * CONV_C, name="proj_rest")

        kn_l.append(heads(ks))
        vn_l.append(heads(vs))
        if l == DEPTH - 1:
            ya, new_k, new_v = _attn_prompt(q, kst, vst, u4, trow, l, batch, seq,
                                            caches=(cache_k, cache_v, jnp.stack(kn_l), jnp.stack(vn_l)))
        else:
            (ya,) = _attn_prompt(q, kst, vst, u4, trow, l, batch, seq)
        yb, cstate = _conv_prompt(u4, conv_w[l], conv_b[l], ln_g[l], ln_b[l], batch, seq)

        yas = _attn_sample(heads(qs.astype(F32)), kn_l[l], vn_l[l], heads(u4s), cache_k, cache_v, tab_s, tab_new, l)
        ybs, cstate_s = _conv_sample(u4s, state_conv[l], conv_w[l], conv_b[l], ln_g[l], ln_b[l])

        y, ys = _mm([ya, yb], [pad(yas).astype(BF16), ybs.astype(BF16)], w_out, l, 0, d, name="proj_out")
        h1, h1b, prs = _post(hp, y, p_all, l, g_post[l], wple_bf[l], 128)
        h1s, h1bs, prss = _post(hs, ys, ps_all, l, g_post[l], wple_bf[l], SAMPLE_ROWS)
        hp, hs = _gate_mm(h1b, h1bs, w_pg, b_pg, l, h1, h1s, p_all, ps_all, wple_bf[l], prs, prss, g_ple[l])
        cp_l.append(cstate)
        cs_l.append(cstate_s)

    kv_shape = (DEPTH, batch, seq, N_HEADS, HEAD_DIM)
    return (hp.reshape(batch, seq, d), hs[:nb].reshape(nb, 1, d),
            kst.reshape(kv_shape), vst.reshape(kv_shape), jnp.stack(cp_l),
            new_k, new_v, jnp.stack(cs_l))
```

```python
import functools
import math

import jax
import jax.numpy as jnp
from jax import lax
from jax.experimental import pallas as pl
from jax.experimental.pallas import tpu as pltpu

D_MODEL = 4096
HEAD_DIM = 128
ATT_W = 2048
N_HEADS = 16
CONV_C = 2048
CONV_K = 31
PLE_DIM = 256
N_BUCKETS = 32
MAX_DISTANCE = 2048
WIN_MAX = 2048
EPS = 1e-6
NEG = -1e30
DEPTH = 2

F32 = jnp.float32
BF16 = jnp.bfloat16

VMEM_LIMIT = 62 * 1024 * 1024
SAMPLE_ROWS = 16
LANE = 128
QBLK = 128
ATT_CHUNK = 256
TAB_W = WIN_MAX + QBLK
TAB_P = TAB_W + QBLK


def _cparams(sem):
    return pltpu.CompilerParams(dimension_semantics=sem, vmem_limit_bytes=VMEM_LIMIT)


def _rmsnorm_kernel(x_ref, g_ref, o_ref):
    x = x_ref[...]
    r = x * lax.rsqrt(jnp.mean(x * x, axis=-1, keepdims=True) + EPS)
    o_ref[...] = (r * g_ref[...]).astype(o_ref.dtype)


def _rmsnorm(x, g, tm):
    m, d = x.shape
    return pl.pallas_call(
        _rmsnorm_kernel,
        grid=(m // tm,),
        in_specs=[pl.BlockSpec((tm, d), lambda i: (i, 0)), pl.BlockSpec((1, d), lambda i: (0, 0))],
        out_specs=pl.BlockSpec((tm, d), lambda i: (i, 0)),
        out_shape=jax.ShapeDtypeStruct((m, d), BF16),
        compiler_params=_cparams(("parallel",)),
        name="rmsnorm",
    )(x, g.reshape(1, d))


MM_TM = 2048
MM_TN = 256
MM_RB = 1024


def _sample_spec(ms, tn):
    return pl.BlockSpec((None, ms, tn), lambda i, j: (i, 0, j))


def _mm_kernel(*refs, nparts, scale, stack):
    xs = refs[:nparts]
    ss = refs[nparts:2 * nparts]
    ws = refs[2 * nparts:3 * nparts]
    no = 3 * nparts + (1 if stack == "update" else 0)
    o_ref, os_ref = refs[no:no + 2]
    wbs = refs[no + 2:]
    for w_ref, wb_ref in zip(ws, wbs):
        wb_ref[...] = w_ref[...].astype(BF16)

    def product(parts, rows):
        acc = None
        for x_ref, wb_ref in zip(parts, wbs):
            d = jnp.dot(x_ref[rows, :], wb_ref[...], preferred_element_type=F32)
            acc = d if acc is None else acc + d
        return acc if scale == 1.0 else acc * scale

    tm = xs[0].shape[0]
    for r in range(0, tm, min(tm, MM_RB)):
        rows = slice(r, r + min(tm, MM_RB))
        res = product(xs, rows).astype(o_ref.dtype)
        if stack == "create":
            o_ref[0, rows, :] = res
            o_ref[1, rows, :] = jnp.zeros(res.shape, res.dtype)
        else:
            o_ref[rows, :] = res

    @pl.when(pl.program_id(0) == 0)
    def _():
        os_ref[...] = product(ss, slice(None)).astype(os_ref.dtype)

    @pl.when(pl.program_id(0) > 0)
    def _():
        os_ref[...] = jnp.zeros(os_ref.shape, os_ref.dtype)


def _mm(x_parts, s_parts, w, layer, col0, ncols, *, scale=1.0, out_dtype=F32, tm=MM_TM, tn=MM_TN, name="mm",
        stack=None):
    nparts = len(x_parts)
    m = x_parts[0].shape[0]
    ms = s_parts[0].shape[0]
    kparts = [x.shape[1] for x in x_parts]
    if isinstance(stack, str):
        tm = tm // 2
    assert sum(kparts) == w.shape[1] and col0 % tn == 0 and ncols % tn == 0 and m % tm == 0
    jb0 = col0 // tn
    koff = [sum(kparts[:p]) for p in range(nparts)]
    in_specs = (
        [pl.BlockSpec((tm, kp), lambda i, j: (i, 0)) for kp in kparts]
        + [pl.BlockSpec((ms, kp), lambda i, j: (0, 0)) for kp in kparts]
        + [pl.BlockSpec((None, kp, tn), functools.partial(lambda i, j, kb: (layer, kb, jb0 + j), kb=ko // kp))
           for kp, ko in zip(kparts, koff)]
    )
    args = [*x_parts, *s_parts, *([w] * nparts)]
    aliases = {}
    if stack is None:
        mode = None
        big_spec = pl.BlockSpec((tm, tn), lambda i, j: (i, j))
        big_shape = jax.ShapeDtypeStruct((m, ncols), out_dtype)
    elif isinstance(stack, str):
        assert stack == "create" and layer == 0 and DEPTH == 2
        mode = "create"
        big_spec = pl.BlockSpec((DEPTH, tm, tn), lambda i, j: (0, i, j))
        big_shape = jax.ShapeDtypeStruct((DEPTH, m, ncols), out_dtype)
    else:
        assert stack.shape == (DEPTH, m, ncols) and stack.dtype == out_dtype
        mode = "update"
        in_specs.append(pl.BlockSpec(memory_space=pl.ANY))
        aliases = {len(args): 0}
        args.append(stack)
        big_spec = pl.BlockSpec((None, tm, tn), lambda i, j: (layer, i, j))
        big_shape = jax.ShapeDtypeStruct((DEPTH, m, ncols), out_dtype)
    out, out_s = pl.pallas_call(
        functools.partial(_mm_kernel, nparts=nparts, scale=scale, stack=mode),
        grid=(m // tm, ncols // tn),
        in_specs=in_specs,
        out_specs=[big_spec, _sample_spec(ms, tn)],
        out_shape=[big_shape, jax.ShapeDtypeStruct((m // tm, ms, ncols), out_dtype)],
        scratch_shapes=[pltpu.VMEM((kp, tn), BF16) for kp in kparts],
        input_output_aliases=aliases,
        compiler_params=_cparams(("arbitrary", "arbitrary")),
        name=name,
    )(*args)
    return out, out_s[0]


def _gate_update(x_ref, wb_ref, b_ref, h_ref, pn_ref, wple_ref, gple_ref, rows):
    acc = jnp.dot(x_ref[rows, :], wb_ref[...], preferred_element_type=F32)
    pen = jnp.dot(pn_ref[rows, :], wple_ref[...], preferred_element_type=F32) * gple_ref[...]
    return h_ref[rows, :] + jax.nn.sigmoid(acc + b_ref[...]) * pen


def _gate_kernel(x_ref, s_ref, w_ref, b_ref, h_ref, hs_ref, pn_ref, pns_ref, wple_ref, gple_ref,
                 o_ref, os_ref, wb_ref):
    wb_ref[...] = w_ref[...].astype(BF16)
    tm = x_ref.shape[0]
    for r in range(0, tm, min(tm, MM_RB)):
        rows = slice(r, r + min(tm, MM_RB))
        o_ref[rows, :] = _gate_update(x_ref, wb_ref, b_ref, h_ref, pn_ref, wple_ref, gple_ref, rows)

    @pl.when(pl.program_id(0) == 0)
    def _():
        os_ref[...] = _gate_update(s_ref, wb_ref, b_ref, hs_ref, pns_ref, wple_ref, gple_ref, slice(None))

    @pl.when(pl.program_id(0) > 0)
    def _():
        os_ref[...] = jnp.zeros(os_ref.shape, os_ref.dtype)


def _gate_mm(x, s, w, b, layer, h, hs, pn, pns, w_ple_bf, g_ple, *, tm=MM_TM, tn=MM_TN):
    m, k = x.shape
    ms = s.shape[0]
    n = w.shape[2]
    big = pl.BlockSpec((tm, tn), lambda i, j: (i, j))
    small = pl.BlockSpec((ms, tn), lambda i, j: (0, j))
    out, out_s = pl.pallas_call(
        _gate_kernel,
        grid=(m // tm, n // tn),
        in_specs=[pl.BlockSpec((tm, k), lambda i, j: (i, 0)), pl.BlockSpec((ms, k), lambda i, j: (0, 0)),
                  pl.BlockSpec((None, k, tn), lambda i, j: (layer, 0, j)),
                  pl.BlockSpec((None, 1, tn), lambda i, j: (layer, 0, j)),
                  big, small,
                  pl.BlockSpec((tm, PLE_DIM), lambda i, j: (i, 0)),
                  pl.BlockSpec((ms, PLE_DIM), lambda i, j: (0, 0)),
                  pl.BlockSpec((PLE_DIM, tn), lambda i, j: (0, j)),
                  pl.BlockSpec((1, tn), lambda i, j: (0, j))],
        out_specs=[big, _sample_spec(ms, tn)],
        out_shape=[jax.ShapeDtypeStruct((m, n), F32), jax.ShapeDtypeStruct((m // tm, ms, n), F32)],
        scratch_shapes=[pltpu.VMEM((k, tn), BF16)],
        compiler_params=_cparams(("arbitrary", "arbitrary")),
        name="gate_mm",
    )(x, s, w, b.reshape(DEPTH, 1, n), h, hs, pn, pns, w_ple_bf, g_ple.reshape(1, n))
    return out, out_s[0]


def _post_kernel(h_ref, y_ref, p_ref, gpost_ref, wple_ref, h1_ref, h1b_ref, pn_ref):
    y = y_ref[...]
    r = y * lax.rsqrt(jnp.mean(y * y, axis=-1, keepdims=True) + EPS) * gpost_ref[...]
    h1 = h_ref[...] + r
    h1_ref[...] = h1
    h1b_ref[...] = h1.astype(BF16)
    p = p_ref[...]
    pe = jnp.dot(p.astype(BF16), wple_ref[...], preferred_element_type=F32)
    rs = lax.rsqrt(jnp.mean(pe * pe, axis=-1, keepdims=True) + EPS)
    pn_ref[...] = (p * rs).astype(BF16)


def _post(h, y, p, layer, g_post, w_ple_bf, tm):
    m, d = h.shape
    row = pl.BlockSpec((tm, d), lambda i: (i, 0))
    vec = pl.BlockSpec((1, d), lambda i: (0, 0))
    return pl.pallas_call(
        _post_kernel,
        grid=(m // tm,),
        in_specs=[row, row, pl.BlockSpec((None, tm, PLE_DIM), lambda i: (layer, i, 0)), vec,
                  pl.BlockSpec((PLE_DIM, d), lambda i: (0, 0))],
        out_specs=[row, row, pl.BlockSpec((tm, PLE_DIM), lambda i: (i, 0))],
        out_shape=[jax.ShapeDtypeStruct((m, d), F32), jax.ShapeDtypeStruct((m, d), BF16),
                   jax.ShapeDtypeStruct((m, PLE_DIM), BF16)],
        compiler_params=_cparams(("parallel",)),
        name="post_norm",
    )(h, y, p, g_post.reshape(1, d), w_ple_bf)


def _shift_window(c_ref, nxt_ref, new_ref, o_ref, is_last):
    t = c_ref.shape[0]
    o_ref[0:t - 1] = c_ref[1:t]
    o_ref[t - 1] = jnp.where(is_last, new_ref[...], nxt_ref[0])


def _attn_kernel(*refs, seq, cache_chunks):
    q_ref, k_ref, v_ref, g_ref, trow_ref = refs[:5]
    hid, bid = pl.program_id(0), pl.program_id(1)
    if cache_chunks:
        ck_ref, ckn_ref, kn_ref, cv_ref, cvn_ref, vn_ref, o_ref, ok_ref, ov_ref = refs[5:14]
        kb_ref, vb_ref, s_ref, tab_ref = refs[14:]
        is_last = (hid * pl.num_programs(1) + bid) % cache_chunks == cache_chunks - 1
        _shift_window(ck_ref, ckn_ref, kn_ref, ok_ref, is_last)
        _shift_window(cv_ref, cvn_ref, vn_ref, ov_ref, is_last)
    else:
        o_ref, kb_ref, vb_ref, s_ref, tab_ref = refs[5:]

    @pl.when(bid == 0)
    def _():
        tab_ref[...] = pltpu.roll(jnp.broadcast_to(trow_ref[...], (QBLK, TAB_P)), 0, 1, stride=1, stride_axis=0)

    kb_ref[...] = k_ref[...].astype(BF16)
    vb_ref[...] = v_ref[...].astype(BF16)
    nblk = seq // QBLK
    for i in range(nblk):
        q = q_ref[i * QBLK:(i + 1) * QBLK, :]
        nkeys = (i + 1) * QBLK
        tab0 = (nblk - i) * QBLK
        chunks = [(c, min(ATT_CHUNK, nkeys - c)) for c in range(0, nkeys, ATT_CHUNK)]
        mx = jnp.full((QBLK, LANE), NEG, F32)
        for c, w in chunks:
            s = lax.dot_general(q, kb_ref[c:c + w, :], (((1,), (1,)), ((), ())), preferred_element_type=F32)
            s = s + tab_ref[:, tab0 + c:tab0 + c + w]
            s_ref[:, c:c + w] = s
            for t in range(0, w, LANE):
                mx = jnp.maximum(mx, s[:, t:t + LANE])
        m = jnp.max(mx, axis=-1, keepdims=True)
        ls = jnp.zeros((QBLK, LANE), F32)
        acc = jnp.zeros((QBLK, HEAD_DIM), F32)
        for c, w in chunks:
            p = jnp.exp(s_ref[:, c:c + w] - m)
            for t in range(0, w, LANE):
                ls = ls + p[:, t:t + LANE]
            acc = acc + jnp.dot(p.astype(BF16), vb_ref[c:c + w, :], preferred_element_type=F32)
        l = jnp.sum(ls, axis=-1, keepdims=True)
        g = g_ref[i * QBLK:(i + 1) * QBLK, :]
        o = acc * (1.0 / l)
        o_ref[i * QBLK:(i + 1) * QBLK, :] = (o * (g * jax.nn.sigmoid(g))).astype(o_ref.dtype)


def _attn_prompt(q, kst, vst, u4, trow, layer, batch, seq, caches=None):
    m = q.shape[0]
    blk = lambda h, b: (b, h)
    kv = pl.BlockSpec((None, seq, HEAD_DIM), lambda h, b: (layer, b, h))
    args = [q, kst, vst, u4, trow]
    in_specs = [pl.BlockSpec((seq, HEAD_DIM), blk), kv, kv, pl.BlockSpec((seq, HEAD_DIM), blk),
                pl.BlockSpec((None, 1, TAB_P), lambda h, b: (h, 0, 0))]
    out_specs = [pl.BlockSpec((seq, HEAD_DIM), blk)]
    out_shape = [jax.ShapeDtypeStruct((m, ATT_W), BF16)]
    cpw = 0
    if caches is not None:
        cache_k, cache_v, knew, vnew = caches
        depth, nb, win = cache_k.shape[:3]
        steps = N_HEADS * batch
        assert (depth * nb * win) % steps == 0
        rows = depth * nb * win // steps
        assert win % rows == 0
        cpw = win // rows
        where = lambda h, b: ((h * batch + b) // (nb * cpw), ((h * batch + b) // cpw) % nb, (h * batch + b) % cpw)
        chunk = pl.BlockSpec((None, None, rows, N_HEADS, HEAD_DIM), lambda h, b: (*where(h, b), 0, 0))
        nxt = pl.BlockSpec((None, None, 1, N_HEADS, HEAD_DIM),
                           lambda h, b: (*where(h, b)[:2], jnp.minimum((where(h, b)[2] + 1) * rows, win - 1), 0, 0))
        new = pl.BlockSpec((None, None, N_HEADS, HEAD_DIM), lambda h, b: (*where(h, b)[:2], 0, 0))
        args += [cache_k, cache_k, knew, cache_v, cache_v, vnew]
        in_specs += [chunk, nxt, new, chunk, nxt, new]
        out_specs += [chunk, chunk]
        out_shape += [jax.ShapeDtypeStruct(cache_k.shape, cache_k.dtype),
                      jax.ShapeDtypeStruct(cache_v.shape, cache_v.dtype)]
    return pl.pallas_call(
        functools.partial(_attn_kernel, seq=seq, cache_chunks=cpw),
        grid=(N_HEADS, batch),
        in_specs=in_specs,
        out_specs=out_specs,
        out_shape=out_shape,
        scratch_shapes=[pltpu.VMEM((seq, HEAD_DIM), BF16), pltpu.VMEM((seq, HEAD_DIM), BF16),
                        pltpu.VMEM((QBLK, seq), F32), pltpu.VMEM((QBLK, TAB_P), F32)],
        compiler_params=_cparams(("arbitrary", "arbitrary")),
        name="attn_prompt",
    )(*args)


CONV_TB = 256
CONV_HIST = 32
CONV_RC = 32
NCH = CONV_C // LANE


def _conv_kernel(ga_ref, gb_ref, gate_ref, w_ref, cb_ref, lg_ref, lb_ref, yb_ref, st_ref, ucat_ref, y_ref):
    t = pl.program_id(1)
    tb = CONV_TB

    @pl.when(t == 0)
    def _():
        ucat_ref[:, 0:CONV_HIST, :] = jnp.zeros((NCH, CONV_HIST, LANE), F32)

    @pl.when(t > 0)
    def _():
        ucat_ref[:, 0:CONV_HIST, :] = ucat_ref[:, tb:tb + CONV_HIST, :]

    for c in range(NCH):
        sl = slice(c * LANE, (c + 1) * LANE)
        ga = ga_ref[:, sl]
        ucat_ref[c, CONV_HIST:CONV_HIST + tb, :] = ga * jax.nn.sigmoid(gb_ref[:, sl])

    off = CONV_HIST - (CONV_K - 1)

    def chunk(c, carry):
        wc = w_ref[c]
        bias = cb_ref[c]
        for r in range(0, tb, CONV_RC):
            acc = jnp.broadcast_to(bias, (CONV_RC, LANE))
            for k in range(CONV_K):
                acc = acc + wc[k:k + 1, :] * ucat_ref[c, r + off + k:r + off + k + CONV_RC, :]
            y_ref[c, r:r + CONV_RC, :] = acc
        return carry

    lax.fori_loop(0, NCH, chunk, 0)

    ssum = jnp.zeros((tb, LANE), F32)
    for c in range(NCH):
        ssum = ssum + y_ref[c]
    mu = jnp.sum(ssum, axis=-1, keepdims=True) * (1.0 / CONV_C)
    sq = jnp.zeros((tb, LANE), F32)
    for c in range(NCH):
        d = y_ref[c] - mu
        sq = sq + d * d
    rstd = lax.rsqrt(jnp.sum(sq, axis=-1, keepdims=True) * (1.0 / CONV_C) + EPS)
    for c in range(NCH):
        sl = slice(c * LANE, (c + 1) * LANE)
        z = (y_ref[c] - mu) * rstd * lg_ref[c] + lb_ref[c]
        g = gate_ref[:, sl]
        yb_ref[:, sl] = (z * jax.nn.sigmoid(z) * (g * jax.nn.sigmoid(g))).astype(yb_ref.dtype)

    @pl.when(t == pl.num_programs(1) - 1)
    def _():
        for c in range(NCH):
            st_ref[:, c * LANE:(c + 1) * LANE] = ucat_ref[c, CONV_HIST + tb - (CONV_K - 1):CONV_HIST + tb, :]


def _chunked(v):
    return v.reshape(v.shape[0], NCH, LANE).transpose(1, 0, 2)


def _conv_prompt(u4, conv_w, conv_b, ln_g, ln_b, batch, seq):
    m = u4.shape[0]
    nt = seq // CONV_TB
    wpad = jnp.zeros((CONV_HIST, CONV_C), F32).at[:CONV_K].set(conv_w)
    col = lambda j: pl.BlockSpec((CONV_TB, CONV_C), lambda b, t, j=j: (b * nt + t, j))
    vec = pl.BlockSpec((NCH, 1, LANE), lambda b, t: (0, 0, 0))
    return pl.pallas_call(
        _conv_kernel,
        grid=(batch, nt),
        in_specs=[col(1), col(2), col(3), pl.BlockSpec((NCH, CONV_HIST, LANE), lambda b, t: (0, 0, 0)), vec, vec, vec],
        out_specs=[pl.BlockSpec((CONV_TB, CONV_C), lambda b, t: (b * nt + t, 0)),
                   pl.BlockSpec((None, CONV_K - 1, CONV_C), lambda b, t: (b, 0, 0))],
        out_shape=[jax.ShapeDtypeStruct((m, CONV_C), BF16),
                   jax.ShapeDtypeStruct((batch, CONV_K - 1, CONV_C), F32)],
        scratch_shapes=[pltpu.VMEM((NCH, CONV_HIST + CONV_TB, LANE), F32), pltpu.VMEM((NCH, CONV_TB, LANE), F32)],
        compiler_params=_cparams(("parallel", "arbitrary")),
        name="conv_prompt",
    )(u4, u4, u4, _chunked(wpad), _chunked(conv_b.reshape(1, -1)), _chunked(ln_g.reshape(1, -1)),
      _chunked(ln_b.reshape(1, -1)))


def _conv_s_kernel(u4_ref, st_ref, w_ref, cb_ref, lg_ref, lb_ref, yb_ref, so_ref, *, nb):
    ga = u4_ref[:, CONV_C:2 * CONV_C]
    u = ga * jax.nn.sigmoid(u4_ref[:, 2 * CONV_C:3 * CONV_C])
    gate = u4_ref[:, 3 * CONV_C:4 * CONV_C]
    yb_ref[...] = jnp.zeros(yb_ref.shape, yb_ref.dtype)
    for b in range(nb):
        st = st_ref[b]
        un = u[b:b + 1, :]
        y = (jnp.sum(st * w_ref[0:CONV_K - 1, :], axis=0, keepdims=True)
             + un * w_ref[CONV_K - 1:CONV_K, :] + cb_ref[...])
        mu = jnp.mean(y, axis=-1, keepdims=True)
        d = y - mu
        z = d * lax.rsqrt(jnp.mean(d * d, axis=-1, keepdims=True) + EPS) * lg_ref[...] + lb_ref[...]
        g = gate[b:b + 1, :]
        yb_ref[b:b + 1, :] = (z * jax.nn.sigmoid(z) * (g * jax.nn.sigmoid(g))).astype(yb_ref.dtype)
        so_ref[b, 0:CONV_K - 2, :] = st[1:CONV_K - 1, :]
        so_ref[b, CONV_K - 2:CONV_K - 1, :] = un


def _conv_sample(u4s, state, conv_w, conv_b, ln_g, ln_b):
    nb = state.shape[0]
    return pl.pallas_call(
        functools.partial(_conv_s_kernel, nb=nb),
        out_shape=[jax.ShapeDtypeStruct((SAMPLE_ROWS, CONV_C), F32),
                   jax.ShapeDtypeStruct(state.shape, F32)],
        compiler_params=pltpu.CompilerParams(vmem_limit_bytes=VMEM_LIMIT),
        name="conv_sample",
    )(u4s, state, conv_w, conv_b.reshape(1, -1), ln_g.reshape(1, -1), ln_b.reshape(1, -1))


def _attn_s_kernel(q_ref, kn_ref, vn_ref, g_ref, k1_ref, k2_ref, k3_ref, v1_ref, v2_ref, v3_ref,
                   tab_ref, tnew_ref, o_ref):
    q = q_ref[...]
    s = [jnp.sum(kr[...] * q[None], axis=-1, keepdims=True) + tab_ref[p]
         for p, kr in enumerate((k1_ref, k2_ref, k3_ref))]
    s_new = jnp.sum(q * kn_ref[...], axis=-1, keepdims=True) + tnew_ref[...]
    m = s_new
    for sp in s:
        m = jnp.maximum(m, jnp.max(sp, axis=0))
    p_new = jnp.exp(s_new - m)
    l = p_new
    acc = p_new * vn_ref[...]
    for sp, vr in zip(s, (v1_ref, v2_ref, v3_ref)):
        p = jnp.exp(sp - m[None])
        l = l + jnp.sum(p, axis=0)
        acc = acc + jnp.sum(p * vr[...], axis=0)
    g = g_ref[...]
    o_ref[...] = acc * (1.0 / l) * (g * jax.nn.sigmoid(g))


def _attn_sample(qs, kn, vn, gs, cache_k, cache_v, tab_s, tab_new, layer):
    nb = qs.shape[0]
    rows = 128
    row = pl.BlockSpec((None, N_HEADS, HEAD_DIM), lambda b: (b, 0, 0))
    c1 = lambda c: c.reshape(DEPTH, nb, WIN_MAX // rows, rows, N_HEADS, HEAD_DIM)
    c2 = lambda c: c.reshape(DEPTH, nb, WIN_MAX // 4, 4, N_HEADS, HEAD_DIM)
    c3 = lambda c: c.reshape(DEPTH, nb, WIN_MAX // 16, 16, N_HEADS, HEAD_DIM)
    s1 = pl.BlockSpec((None, None, None, rows, N_HEADS, HEAD_DIM), lambda b: (layer, b, WIN_MAX // rows - 1, 0, 0, 0))
    s2 = pl.BlockSpec((None, None, rows, None, N_HEADS, HEAD_DIM), lambda b: (layer, b, 3, 0, 0, 0))
    s3 = pl.BlockSpec((None, None, rows, None, N_HEADS, HEAD_DIM), lambda b: (layer, b, 0, 0, 0, 0))
    return pl.pallas_call(
        _attn_s_kernel,
        grid=(nb,),
        in_specs=[row, row, row, row, s1, s2, s3, s1, s2, s3,
                  pl.BlockSpec((3, rows, N_HEADS, 1), lambda b: (0, 0, 0, 0)),
                  pl.BlockSpec((N_HEADS, 1), lambda b: (0, 0))],
        out_specs=row,
        out_shape=jax.ShapeDtypeStruct((nb, N_HEADS, HEAD_DIM), F32),
        compiler_params=_cparams(("parallel",)),
        name="attn_sample",
    )(qs, kn, vn, gs, c1(cache_k), c2(cache_k), c3(cache_k), c1(cache_v), c2(cache_v), c3(cache_v),
      tab_s, tab_new)


def _rel_bucket(dist):
    max_exact = N_BUCKETS // 2
    df = jnp.maximum(dist, 1).astype(F32)
    large = max_exact + (jnp.log(df / max_exact) / math.log(MAX_DISTANCE / max_exact)
                         * (N_BUCKETS - max_exact)).astype(jnp.int32)
    large = jnp.minimum(large, N_BUCKETS - 1)
    return jnp.where(dist < max_exact, dist, large)


def _bias_tables(rel_bias):
    dist = jnp.arange(WIN_MAX + 1, dtype=jnp.int32)
    bias = rel_bias[_rel_bucket(dist)].astype(F32).T
    mult = ((dist <= 128).astype(F32) + ((dist % 4 == 0) & (dist <= 512)).astype(F32)
            + (dist % 16 == 0).astype(F32))
    t1d = jnp.where(mult > 0, bias + jnp.log(jnp.maximum(mult, 1.0)), NEG)
    trow = jnp.full((N_HEADS, 1, TAB_P), NEG, F32).at[:, 0, :WIN_MAX + 1].set(t1d[:, ::-1])
    j = jnp.arange(128, dtype=jnp.int32)
    d_rows = jnp.stack([128 - j, 512 - 4 * j, 2048 - 16 * j])
    tab_s = bias[:, d_rows].transpose(1, 2, 0)[..., None]
    tab_new = (bias[:, 0] + math.log(3.0)).reshape(N_HEADS, 1)
    return trow, tab_s, tab_new


def kernel(x_prompt, x_sample, cache_k, cache_v, state_conv, p_prompt, p_sample, rel_bias, g_pre, w_in, conv_w,
           conv_b, ln_g, ln_b, w_out, g_post, w_ple, g_ple, w_pg, b_pg):
    batch, seq, d = x_prompt.shape
    nb = x_sample.shape[0]
    m = batch * seq
    pad = lambda a: jnp.zeros((SAMPLE_ROWS, a.size // nb), a.dtype).at[:nb].set(a.reshape(nb, -1))
    heads = lambda a: a[:nb, :ATT_W].reshape(nb, N_HEADS, HEAD_DIM)

    hp = x_prompt.reshape(m, d)
    hs = pad(x_sample)
    trow, tab_s, tab_new = _bias_tables(rel_bias)
    p_all = p_prompt.reshape(DEPTH, m, PLE_DIM)
    ps_all = jnp.stack([pad(p_sample[l]) for l in range(DEPTH)])
    wple_bf = w_ple.astype(BF16)
    kst = vst = new_k = new_v = None
    cp_l, cs_l, kn_l, vn_l = [], [], [], []
    for l in range(DEPTH):
        xn = _rmsnorm(hp, g_pre[l], 256)
        xns = _rmsnorm(hs, g_pre[l], SAMPLE_ROWS)
        q, qs = _mm([xn], [xns], w_in, l, 0, ATT_W, scale=HEAD_DIM ** -0.5, out_dtype=BF16, name="proj_q")
        kst, ks = _mm([xn], [xns], w_in, l, ATT_W, ATT_W, name="proj_k", stack="create" if l == 0 else kst)
        vst, vs = _mm([xn], [xns], w_in, l, 2 * ATT_W, ATT_W, name="proj_v", stack="create" if l == 0 else vst)
        u4, u4s = _mm([xn], [xns], w_in, l, 3 * ATT_W, ATT_W + 3 * CONV_C, name="proj_rest")

        kn_l.append(heads(ks))
        vn_l.append(heads(vs))
        if l == DEPTH - 1:
            ya, new_k, new_v = _attn_prompt(q, kst, vst, u4, trow, l, batch, seq,
                                            caches=(cache_k, cache_v, jnp.stack(kn_l), jnp.stack(vn_l)))
        else:
            (ya,) = _attn_prompt(q, kst, vst, u4, trow, l, batch, seq)
        yb, cstate = _conv_prompt(u4, conv_w[l], conv_b[l], ln_g[l], ln_b[l], batch, seq)

        yas = _attn_sample(heads(qs.astype(F32)), kn_l[l], vn_l[l], heads(u4s), cache_k, cache_v, tab_s, tab_new, l)
        ybs, cstate_s = _conv_sample(u4s, state_conv[l], conv_w[l], conv_b[l], ln_g[l], ln_b[l])

        y, ys = _mm([ya, yb], [pad(yas).astype(BF16), ybs.astype(BF16)], w_out, l, 0, d, name="proj_out")
        h1, h1b, pn = _post(hp, y, p_all, l, g_post[l], wple_bf[l], 128)
        h1s, h1bs, pns = _post(hs, ys, ps_all, l, g_post[l], wple_bf[l], SAMPLE_ROWS)
        hp, hs = _gate_mm(h1b, h1bs, w_pg, b_pg, l, h1, h1s, pn, pns, wple_bf[l], g_ple[l])
        cp_l.append(cstate)
        cs_l.append(cstate_s)

    kv_shape = (DEPTH, batch, seq, N_HEADS, HEAD_DIM)
    return (hp.reshape(batch, seq, d), hs[:nb].reshape(nb, 1, d),
            kst.reshape(kv_shape), vst.reshape(kv_shape), jnp.stack(cp_l),
            new_k, new_v, jnp.stack(cs_l))
```

```python
import functools
import math

import jax
import jax.numpy as jnp
from jax import lax
from jax.experimental import pallas as pl
from jax.experimental.pallas import tpu as pltpu

D_MODEL = 4096
HEAD_DIM = 128
ATT_W = 2048
N_HEADS = 16
CONV_C = 2048
CONV_K = 31
PLE_DIM = 256
N_BUCKETS = 32
MAX_DISTANCE = 2048
WIN_MAX = 2048
EPS = 1e-6
NEG = -1e30
LOG2E = math.log2(math.e)
DEPTH = 2

F32 = jnp.float32
BF16 = jnp.bfloat16

VMEM_LIMIT = 62 * 1024 * 1024
SAMPLE_ROWS = 16
LANE = 128
QBLK = 256
ATT_CHUNK = 512
ATT_SBUF = 2
TAB_W = WIN_MAX + QBLK
TAB_P = TAB_W + QBLK


def _cparams(sem):
    return pltpu.CompilerParams(dimension_semantics=sem, vmem_limit_bytes=VMEM_LIMIT)


def _rmsnorm_kernel(x_ref, g_ref, o_ref):
    x = x_ref[...]
    r = x * lax.rsqrt(jnp.mean(x * x, axis=-1, keepdims=True) + EPS)
    o_ref[...] = (r * g_ref[...]).astype(o_ref.dtype)


def _rmsnorm(x, g, tm):
    m, d = x.shape
    return pl.pallas_call(
        _rmsnorm_kernel,
        grid=(m // tm,),
        in_specs=[pl.BlockSpec((tm, d), lambda i: (i, 0)), pl.BlockSpec((1, d), lambda i: (0, 0))],
        out_specs=pl.BlockSpec((tm, d), lambda i: (i, 0)),
        out_shape=jax.ShapeDtypeStruct((m, d), BF16),
        compiler_params=_cparams(("parallel",)),
        name="rmsnorm",
    )(x, g.reshape(1, d))


MM_TM = 2048
MM_TN = 256
MM_RB = 1024


def _sample_spec(ms, tn):
    return pl.BlockSpec((None, ms, tn), lambda i, j: (i, 0, j))


def _mm_kernel(*refs, nparts, scale, stack):
    xs = refs[:nparts]
    ss = refs[nparts:2 * nparts]
    ws = refs[2 * nparts:3 * nparts]
    no = 3 * nparts + (1 if stack == "update" else 0)
    o_ref, os_ref = refs[no:no + 2]
    wbs = refs[no + 2:]
    for w_ref, wb_ref in zip(ws, wbs):
        wb_ref[...] = w_ref[...].astype(BF16)

    def product(parts, rows):
        acc = None
        for x_ref, wb_ref in zip(parts, wbs):
            d = jnp.dot(x_ref[rows, :], wb_ref[...], preferred_element_type=F32)
            acc = d if acc is None else acc + d
        return acc if scale == 1.0 else acc * scale

    tm = xs[0].shape[0]
    for r in range(0, tm, min(tm, MM_RB)):
        rows = slice(r, r + min(tm, MM_RB))
        res = product(xs, rows).astype(o_ref.dtype)
        if stack == "create":
            o_ref[0, rows, :] = res
            o_ref[1, rows, :] = jnp.zeros(res.shape, res.dtype)
        else:
            o_ref[rows, :] = res

    @pl.when(pl.program_id(0) == 0)
    def _():
        os_ref[...] = product(ss, slice(None)).astype(os_ref.dtype)

    @pl.when(pl.program_id(0) > 0)
    def _():
        os_ref[...] = jnp.zeros(os_ref.shape, os_ref.dtype)


def _mm(x_parts, s_parts, w, layer, col0, ncols, *, scale=1.0, out_dtype=F32, tm=MM_TM, tn=MM_TN, name="mm",
        stack=None):
    nparts = len(x_parts)
    m = x_parts[0].shape[0]
    ms = s_parts[0].shape[0]
    kparts = [x.shape[1] for x in x_parts]
    assert sum(kparts) == w.shape[1] and col0 % tn == 0 and ncols % tn == 0 and m % tm == 0
    jb0 = col0 // tn
    koff = [sum(kparts[:p]) for p in range(nparts)]
    in_specs = (
        [pl.BlockSpec((tm, kp), lambda i, j: (i, 0)) for kp in kparts]
        + [pl.BlockSpec((ms, kp), lambda i, j: (0, 0)) for kp in kparts]
        + [pl.BlockSpec((None, kp, tn), functools.partial(lambda i, j, kb: (layer, kb, jb0 + j), kb=ko // kp))
           for kp, ko in zip(kparts, koff)]
    )
    args = [*x_parts, *s_parts, *([w] * nparts)]
    aliases = {}
    if stack is None:
        mode = None
        big_spec = pl.BlockSpec((tm, tn), lambda i, j: (i, j))
        big_shape = jax.ShapeDtypeStruct((m, ncols), out_dtype)
    elif isinstance(stack, str):
        assert stack == "create" and layer == 0 and DEPTH == 2
        mode = "create"
        big_spec = pl.BlockSpec((DEPTH, tm, tn), lambda i, j: (0, i, j))
        big_shape = jax.ShapeDtypeStruct((DEPTH, m, ncols), out_dtype)
    else:
        assert stack.shape == (DEPTH, m, ncols) and stack.dtype == out_dtype
        mode = "update"
        in_specs.append(pl.BlockSpec(memory_space=pl.ANY))
        aliases = {len(args): 0}
        args.append(stack)
        big_spec = pl.BlockSpec((None, tm, tn), lambda i, j: (layer, i, j))
        big_shape = jax.ShapeDtypeStruct((DEPTH, m, ncols), out_dtype)
    out, out_s = pl.pallas_call(
        functools.partial(_mm_kernel, nparts=nparts, scale=scale, stack=mode),
        grid=(m // tm, ncols // tn),
        in_specs=in_specs,
        out_specs=[big_spec, _sample_spec(ms, tn)],
        out_shape=[big_shape, jax.ShapeDtypeStruct((m // tm, ms, ncols), out_dtype)],
        scratch_shapes=[pltpu.VMEM((kp, tn), BF16) for kp in kparts],
        input_output_aliases=aliases,
        compiler_params=_cparams(("arbitrary", "arbitrary")),
        name=name,
    )(*args)
    return out, out_s[0]


def _gate_update(x_ref, wb_ref, b_ref, h_ref, pn_ref, wple_ref, gple_ref, rows):
    acc = jnp.dot(x_ref[rows, :], wb_ref[...], preferred_element_type=F32)
    pen = jnp.dot(pn_ref[rows, :], wple_ref[...], preferred_element_type=F32) * gple_ref[...]
    return h_ref[rows, :] + jax.nn.sigmoid(acc + b_ref[...]) * pen


def _gate_kernel(x_ref, s_ref, w_ref, b_ref, h_ref, hs_ref, pn_ref, pns_ref, wple_ref, gple_ref,
                 o_ref, os_ref, wb_ref):
    wb_ref[...] = w_ref[...].astype(BF16)
    tm = x_ref.shape[0]
    for r in range(0, tm, min(tm, MM_RB)):
        rows = slice(r, r + min(tm, MM_RB))
        o_ref[rows, :] = _gate_update(x_ref, wb_ref, b_ref, h_ref, pn_ref, wple_ref, gple_ref, rows)

    @pl.when(pl.program_id(0) == 0)
    def _():
        os_ref[...] = _gate_update(s_ref, wb_ref, b_ref, hs_ref, pns_ref, wple_ref, gple_ref, slice(None))

    @pl.when(pl.program_id(0) > 0)
    def _():
        os_ref[...] = jnp.zeros(os_ref.shape, os_ref.dtype)


def _gate_mm(x, s, w, b, layer, h, hs, pn, pns, w_ple_bf, g_ple, *, tm=MM_TM, tn=MM_TN):
    m, k = x.shape
    ms = s.shape[0]
    n = w.shape[2]
    big = pl.BlockSpec((tm, tn), lambda i, j: (i, j))
    small = pl.BlockSpec((ms, tn), lambda i, j: (0, j))
    out, out_s = pl.pallas_call(
        _gate_kernel,
        grid=(m // tm, n // tn),
        in_specs=[pl.BlockSpec((tm, k), lambda i, j: (i, 0)), pl.BlockSpec((ms, k), lambda i, j: (0, 0)),
                  pl.BlockSpec((None, k, tn), lambda i, j: (layer, 0, j)),
                  pl.BlockSpec((None, 1, tn), lambda i, j: (layer, 0, j)),
                  big, small,
                  pl.BlockSpec((tm, PLE_DIM), lambda i, j: (i, 0)),
                  pl.BlockSpec((ms, PLE_DIM), lambda i, j: (0, 0)),
                  pl.BlockSpec((PLE_DIM, tn), lambda i, j: (0, j)),
                  pl.BlockSpec((1, tn), lambda i, j: (0, j))],
        out_specs=[big, _sample_spec(ms, tn)],
        out_shape=[jax.ShapeDtypeStruct((m, n), F32), jax.ShapeDtypeStruct((m // tm, ms, n), F32)],
        scratch_shapes=[pltpu.VMEM((k, tn), BF16)],
        compiler_params=_cparams(("arbitrary", "arbitrary")),
        name="gate_mm",
    )(x, s, w, b.reshape(DEPTH, 1, n), h, hs, pn, pns, w_ple_bf, g_ple.reshape(1, n))
    return out, out_s[0]


def _post_kernel(h_ref, y_ref, p_ref, gpost_ref, wple_ref, h1_ref, h1b_ref, pn_ref):
    y = y_ref[...]
    r = y * lax.rsqrt(jnp.mean(y * y, axis=-1, keepdims=True) + EPS) * gpost_ref[...]
    h1 = h_ref[...] + r
    h1_ref[...] = h1
    h1b_ref[...] = h1.astype(BF16)
    p = p_ref[...]
    pe = jnp.dot(p.astype(BF16), wple_ref[...], preferred_element_type=F32)
    rs = lax.rsqrt(jnp.mean(pe * pe, axis=-1, keepdims=True) + EPS)
    pn_ref[...] = (p * rs).astype(BF16)


def _post(h, y, p, layer, g_post, w_ple_bf, tm):
    m, d = h.shape
    row = pl.BlockSpec((tm, d), lambda i: (i, 0))
    vec = pl.BlockSpec((1, d), lambda i: (0, 0))
    return pl.pallas_call(
        _post_kernel,
        grid=(m // tm,),
        in_specs=[row, row, pl.BlockSpec((None, tm, PLE_DIM), lambda i: (layer, i, 0)), vec,
                  pl.BlockSpec((PLE_DIM, d), lambda i: (0, 0))],
        out_specs=[row, row, pl.BlockSpec((tm, PLE_DIM), lambda i: (i, 0))],
        out_shape=[jax.ShapeDtypeStruct((m, d), F32), jax.ShapeDtypeStruct((m, d), BF16),
                   jax.ShapeDtypeStruct((m, PLE_DIM), BF16)],
        compiler_params=_cparams(("parallel",)),
        name="post_norm",
    )(h, y, p, g_post.reshape(1, d), w_ple_bf)


def _shift_window(c_ref, nxt_ref, new_ref, o_ref, is_last):
    t = c_ref.shape[0]
    o_ref[0:t - 1] = c_ref[1:t]
    o_ref[t - 1] = jnp.where(is_last, new_ref[...], nxt_ref[0])


def _attn_kernel(*refs, seq, cache_chunks):
    q_ref, k_ref, v_ref, g_ref, trow_ref = refs[:5]
    hid, bid = pl.program_id(0), pl.program_id(1)
    if cache_chunks:
        ck_ref, ckn_ref, kn_ref, cv_ref, cvn_ref, vn_ref, o_ref, ok_ref, ov_ref = refs[5:14]
        kb_ref, vb_ref, s_ref, tab_ref = refs[14:]
        is_last = (hid * pl.num_programs(1) + bid) % cache_chunks == cache_chunks - 1
        _shift_window(ck_ref, ckn_ref, kn_ref, ok_ref, is_last)
        _shift_window(cv_ref, cvn_ref, vn_ref, ov_ref, is_last)
    else:
        o_ref, kb_ref, vb_ref, s_ref, tab_ref = refs[5:]

    @pl.when(bid == 0)
    def _():
        tab_ref[...] = pltpu.roll(jnp.broadcast_to(trow_ref[...], (QBLK, TAB_P)), 0, 1, stride=1, stride_axis=0)

    kb_ref[...] = k_ref[...].astype(BF16)
    vb_ref[...] = v_ref[...].astype(BF16)
    nblk = seq // QBLK
    for i in range(nblk):
        q = q_ref[i * QBLK:(i + 1) * QBLK, :]
        nkeys = (i + 1) * QBLK
        tab0 = (nblk - i) * QBLK
        chunks = [(c, min(ATT_CHUNK, nkeys - c)) for c in range(0, nkeys, ATT_CHUNK)]
        mx = jnp.full((QBLK, LANE), NEG, F32)
        for c, w in chunks:
            s = lax.dot_general(q, kb_ref[c:c + w, :], (((1,), (1,)), ((), ())), preferred_element_type=F32)
            s = s + tab_ref[:, tab0 + c:tab0 + c + w]
            s_ref[i % ATT_SBUF, :, c:c + w] = s
            for t in range(0, w, LANE):
                mx = jnp.maximum(mx, s[:, t:t + LANE])
        m = jnp.max(mx, axis=-1, keepdims=True)
        ls = jnp.zeros((QBLK, LANE), F32)
        acc = jnp.zeros((QBLK, HEAD_DIM), F32)
        for c, w in chunks:
            p = jnp.exp2(s_ref[i % ATT_SBUF, :, c:c + w] - m)
            for t in range(0, w, LANE):
                ls = ls + p[:, t:t + LANE]
            acc = acc + jnp.dot(p.astype(BF16), vb_ref[c:c + w, :], preferred_element_type=F32)
        l = jnp.sum(ls, axis=-1, keepdims=True)
        g = g_ref[i * QBLK:(i + 1) * QBLK, :]
        o = acc * (1.0 / l)
        o_ref[i * QBLK:(i + 1) * QBLK, :] = (o * (g * jax.nn.sigmoid(g))).astype(o_ref.dtype)


def _attn_prompt(q, kst, vst, u4, trow, layer, batch, seq, caches=None):
    m = q.shape[0]
    blk = lambda h, b: (b, h)
    kv = pl.BlockSpec((None, seq, HEAD_DIM), lambda h, b: (layer, b, h))
    args = [q, kst, vst, u4, trow]
    in_specs = [pl.BlockSpec((seq, HEAD_DIM), blk), kv, kv, pl.BlockSpec((seq, HEAD_DIM), blk),
                pl.BlockSpec((None, 1, TAB_P), lambda h, b: (h, 0, 0))]
    out_specs = [pl.BlockSpec((seq, HEAD_DIM), blk)]
    out_shape = [jax.ShapeDtypeStruct((m, ATT_W), BF16)]
    cpw = 0
    if caches is not None:
        cache_k, cache_v, knew, vnew = caches
        depth, nb, win = cache_k.shape[:3]
        steps = N_HEADS * batch
        assert (depth * nb * win) % steps == 0
        rows = depth * nb * win // steps
        assert win % rows == 0
        cpw = win // rows
        where = lambda h, b: ((h * batch + b) // (nb * cpw), ((h * batch + b) // cpw) % nb, (h * batch + b) % cpw)
        chunk = pl.BlockSpec((None, None, rows, N_HEADS, HEAD_DIM), lambda h, b: (*where(h, b), 0, 0))
        nxt = pl.BlockSpec((None, None, 1, N_HEADS, HEAD_DIM),
                           lambda h, b: (*where(h, b)[:2], jnp.minimum((where(h, b)[2] + 1) * rows, win - 1), 0, 0))
        new = pl.BlockSpec((None, None, N_HEADS, HEAD_DIM), lambda h, b: (*where(h, b)[:2], 0, 0))
        args += [cache_k, cache_k, knew, cache_v, cache_v, vnew]
        in_specs += [chunk, nxt, new, chunk, nxt, new]
        out_specs += [chunk, chunk]
        out_shape += [jax.ShapeDtypeStruct(cache_k.shape, cache_k.dtype),
                      jax.ShapeDtypeStruct(cache_v.shape, cache_v.dtype)]
    return pl.pallas_call(
        functools.partial(_attn_kernel, seq=seq, cache_chunks=cpw),
        grid=(N_HEADS, batch),
        in_specs=in_specs,
        out_specs=out_specs,
        out_shape=out_shape,
        scratch_shapes=[pltpu.VMEM((seq, HEAD_DIM), BF16), pltpu.VMEM((seq, HEAD_DIM), BF16),
                        pltpu.VMEM((ATT_SBUF, QBLK, seq), F32), pltpu.VMEM((QBLK, TAB_P), F32)],
        compiler_params=_cparams(("arbitrary", "arbitrary")),
        name="attn_prompt",
    )(*args)


CONV_TB = 256
CONV_HIST = 32
CONV_RC = 32
NCH = CONV_C // LANE


def _conv_kernel(ga_ref, gb_ref, gate_ref, w_ref, cb_ref, lg_ref, lb_ref, yb_ref, st_ref, ucat_ref, y_ref):
    t = pl.program_id(1)
    tb = CONV_TB

    @pl.when(t == 0)
    def _():
        ucat_ref[:, 0:CONV_HIST, :] = jnp.zeros((NCH, CONV_HIST, LANE), F32)

    @pl.when(t > 0)
    def _():
        ucat_ref[:, 0:CONV_HIST, :] = ucat_ref[:, tb:tb + CONV_HIST, :]

    for c in range(NCH):
        sl = slice(c * LANE, (c + 1) * LANE)
        ga = ga_ref[:, sl]
        ucat_ref[c, CONV_HIST:CONV_HIST + tb, :] = ga * jax.nn.sigmoid(gb_ref[:, sl])

    off = CONV_HIST - (CONV_K - 1)

    def chunk(c, carry):
        wc = w_ref[c]
        bias = cb_ref[c]
        for r in range(0, tb, CONV_RC):
            acc = jnp.broadcast_to(bias, (CONV_RC, LANE))
            for k in range(CONV_K):
                acc = acc + wc[k:k + 1, :] * ucat_ref[c, r + off + k:r + off + k + CONV_RC, :]
            y_ref[c, r:r + CONV_RC, :] = acc
        return carry

    lax.fori_loop(0, NCH, chunk, 0)

    ssum = jnp.zeros((tb, LANE), F32)
    for c in range(NCH):
        ssum = ssum + y_ref[c]
    mu = jnp.sum(ssum, axis=-1, keepdims=True) * (1.0 / CONV_C)
    sq = jnp.zeros((tb, LANE), F32)
    for c in range(NCH):
        d = y_ref[c] - mu
        sq = sq + d * d
    rstd = lax.rsqrt(jnp.sum(sq, axis=-1, keepdims=True) * (1.0 / CONV_C) + EPS)
    for c in range(NCH):
        sl = slice(c * LANE, (c + 1) * LANE)
        z = (y_ref[c] - mu) * rstd * lg_ref[c] + lb_ref[c]
        g = gate_ref[:, sl]
        yb_ref[:, sl] = (z * jax.nn.sigmoid(z) * (g * jax.nn.sigmoid(g))).astype(yb_ref.dtype)

    @pl.when(t == pl.num_programs(1) - 1)
    def _():
        for c in range(NCH):
            st_ref[:, c * LANE:(c + 1) * LANE] = ucat_ref[c, CONV_HIST + tb - (CONV_K - 1):CONV_HIST + tb, :]


def _chunked(v):
    return v.reshape(v.shape[0], NCH, LANE).transpose(1, 0, 2)


def _conv_prompt(u4, conv_w, conv_b, ln_g, ln_b, batch, seq):
    m = u4.shape[0]
    nt = seq // CONV_TB
    wpad = jnp.zeros((CONV_HIST, CONV_C), F32).at[:CONV_K].set(conv_w)
    col = lambda j: pl.BlockSpec((CONV_TB, CONV_C), lambda b, t, j=j: (b * nt + t, j))
    vec = pl.BlockSpec((NCH, 1, LANE), lambda b, t: (0, 0, 0))
    return pl.pallas_call(
        _conv_kernel,
        grid=(batch, nt),
        in_specs=[col(1), col(2), col(3), pl.BlockSpec((NCH, CONV_HIST, LANE), lambda b, t: (0, 0, 0)), vec, vec, vec],
        out_specs=[pl.BlockSpec((CONV_TB, CONV_C), lambda b, t: (b * nt + t, 0)),
                   pl.BlockSpec((None, CONV_K - 1, CONV_C), lambda b, t: (b, 0, 0))],
        out_shape=[jax.ShapeDtypeStruct((m, CONV_C), BF16),
                   jax.ShapeDtypeStruct((batch, CONV_K - 1, CONV_C), F32)],
        scratch_shapes=[pltpu.VMEM((NCH, CONV_HIST + CONV_TB, LANE), F32), pltpu.VMEM((NCH, CONV_TB, LANE), F32)],
        compiler_params=_cparams(("parallel", "arbitrary")),
        name="conv_prompt",
    )(u4, u4, u4, _chunked(wpad), _chunked(conv_b.reshape(1, -1)), _chunked(ln_g.reshape(1, -1)),
      _chunked(ln_b.reshape(1, -1)))


def _conv_s_kernel(u4_ref, st_ref, w_ref, cb_ref, lg_ref, lb_ref, yb_ref, so_ref, *, nb):
    ga = u4_ref[:, CONV_C:2 * CONV_C]
    u = ga * jax.nn.sigmoid(u4_ref[:, 2 * CONV_C:3 * CONV_C])
    gate = u4_ref[:, 3 * CONV_C:4 * CONV_C]
    yb_ref[...] = jnp.zeros(yb_ref.shape, yb_ref.dtype)
    for b in range(nb):
        st = st_ref[b]
        un = u[b:b + 1, :]
        y = (jnp.sum(st * w_ref[0:CONV_K - 1, :], axis=0, keepdims=True)
             + un * w_ref[CONV_K - 1:CONV_K, :] + cb_ref[...])
        mu = jnp.mean(y, axis=-1, keepdims=True)
        d = y - mu
        z = d * lax.rsqrt(jnp.mean(d * d, axis=-1, keepdims=True) + EPS) * lg_ref[...] + lb_ref[...]
        g = gate[b:b + 1, :]
        yb_ref[b:b + 1, :] = (z * jax.nn.sigmoid(z) * (g * jax.nn.sigmoid(g))).astype(yb_ref.dtype)
        so_ref[b, 0:CONV_K - 2, :] = st[1:CONV_K - 1, :]
        so_ref[b, CONV_K - 2:CONV_K - 1, :] = un


def _conv_sample(u4s, state, conv_w, conv_b, ln_g, ln_b):
    nb = state.shape[0]
    return pl.pallas_call(
        functools.partial(_conv_s_kernel, nb=nb),
        out_shape=[jax.ShapeDtypeStruct((SAMPLE_ROWS, CONV_C), F32),
                   jax.ShapeDtypeStruct(state.shape, F32)],
        compiler_params=pltpu.CompilerParams(vmem_limit_bytes=VMEM_LIMIT),
        name="conv_sample",
    )(u4s, state, conv_w, conv_b.reshape(1, -1), ln_g.reshape(1, -1), ln_b.reshape(1, -1))


def _attn_s_kernel(q_ref, kn_ref, vn_ref, g_ref, k1_ref, k2_ref, k3_ref, v1_ref, v2_ref, v3_ref,
                   tab_ref, tnew_ref, o_ref):
    q = q_ref[...]
    s = [jnp.sum(kr[...] * q[None], axis=-1, keepdims=True) + tab_ref[p]
         for p, kr in enumerate((k1_ref, k2_ref, k3_ref))]
    s_new = jnp.sum(q * kn_ref[...], axis=-1, keepdims=True) + tnew_ref[...]
    m = s_new
    for sp in s:
        m = jnp.maximum(m, jnp.max(sp, axis=0))
    p_new = jnp.exp2(s_new - m)
    l = p_new
    acc = p_new * vn_ref[...]
    for sp, vr in zip(s, (v1_ref, v2_ref, v3_ref)):
        p = jnp.exp2(sp - m[None])
        l = l + jnp.sum(p, axis=0)
        acc = acc + jnp.sum(p * vr[...], axis=0)
    g = g_ref[...]
    o_ref[...] = acc * (1.0 / l) * (g * jax.nn.sigmoid(g))


def _attn_sample(qs, kn, vn, gs, cache_k, cache_v, tab_s, tab_new, layer):
    nb = qs.shape[0]
    rows = 128
    row = pl.BlockSpec((None, N_HEADS, HEAD_DIM), lambda b: (b, 0, 0))
    c1 = lambda c: c.reshape(DEPTH, nb, WIN_MAX // rows, rows, N_HEADS, HEAD_DIM)
    c2 = lambda c: c.reshape(DEPTH, nb, WIN_MAX // 4, 4, N_HEADS, HEAD_DIM)
    c3 = lambda c: c.reshape(DEPTH, nb, WIN_MAX // 16, 16, N_HEADS, HEAD_DIM)
    s1 = pl.BlockSpec((None, None, None, rows, N_HEADS, HEAD_DIM), lambda b: (layer, b, WIN_MAX // rows - 1, 0, 0, 0))
    s2 = pl.BlockSpec((None, None, rows, None, N_HEADS, HEAD_DIM), lambda b: (layer, b, 3, 0, 0, 0))
    s3 = pl.BlockSpec((None, None, rows, None, N_HEADS, HEAD_DIM), lambda b: (layer, b, 0, 0, 0, 0))
    return pl.pallas_call(
        _attn_s_kernel,
        grid=(nb,),
        in_specs=[row, row, row, row, s1, s2, s3, s1, s2, s3,
                  pl.BlockSpec((3, rows, N_HEADS, 1), lambda b: (0, 0, 0, 0)),
                  pl.BlockSpec((N_HEADS, 1), lambda b: (0, 0))],
        out_specs=row,
        out_shape=jax.ShapeDtypeStruct((nb, N_HEADS, HEAD_DIM), F32),
        compiler_params=_cparams(("parallel",)),
        name="attn_sample",
    )(qs, kn, vn, gs, c1(cache_k), c2(cache_k), c3(cache_k), c1(cache_v), c2(cache_v), c3(cache_v),
      tab_s, tab_new)


def _rel_bucket(dist):
    max_exact = N_BUCKETS // 2
    df = jnp.maximum(dist, 1).astype(F32)
    large = max_exact + (jnp.log(df / max_exact) / math.log(MAX_DISTANCE / max_exact)
                         * (N_BUCKETS - max_exact)).astype(jnp.int32)
    large = jnp.minimum(large, N_BUCKETS - 1)
    return jnp.where(dist < max_exact, dist, large)


def _bias_tables(rel_bias):
    dist = jnp.arange(WIN_MAX + 1, dtype=jnp.int32)
    bias = rel_bias[_rel_bucket(dist)].astype(F32).T
    mult = ((dist <= 128).astype(F32) + ((dist % 4 == 0) & (dist <= 512)).astype(F32)
            + (dist % 16 == 0).astype(F32))
    t1d = jnp.where(mult > 0, (bias + jnp.log(jnp.maximum(mult, 1.0))) * LOG2E, NEG)
    trow = jnp.full((N_HEADS, 1, TAB_P), NEG, F32).at[:, 0, :WIN_MAX + 1].set(t1d[:, ::-1])
    j = jnp.arange(128, dtype=jnp.int32)
    d_rows = jnp.stack([128 - j, 512 - 4 * j, 2048 - 16 * j])
    tab_s = (bias[:, d_rows] * LOG2E).transpose(1, 2, 0)[..., None]
    tab_new = ((bias[:, 0] + math.log(3.0)) * LOG2E).reshape(N_HEADS, 1)
    return trow, tab_s, tab_new


def kernel(x_prompt, x_sample, cache_k, cache_v, state_conv, p_prompt, p_sample, rel_bias, g_pre, w_in, conv_w,
           conv_b, ln_g, ln_b, w_out, g_post, w_ple, g_ple, w_pg, b_pg):
    batch, seq, d = x_prompt.shape
    nb = x_sample.shape[0]
    m = batch * seq
    pad = lambda a: jnp.zeros((SAMPLE_ROWS, a.size // nb), a.dtype).at[:nb].set(a.reshape(nb, -1))
    heads = lambda a: a[:nb, :ATT_W].reshape(nb, N_HEADS, HEAD_DIM)

    hp = x_prompt.reshape(m, d)
    hs = pad(x_sample)
    trow, tab_s, tab_new = _bias_tables(rel_bias)
    p_all = p_prompt.reshape(DEPTH, m, PLE_DIM)
    ps_all = jnp.stack([pad(p_sample[l]) for l in range(DEPTH)])
    wple_bf = w_ple.astype(BF16)
    kst = vst = new_k = new_v = None
    cp_l, cs_l, kn_l, vn_l = [], [], [], []
    for l in range(DEPTH):
        xn = _rmsnorm(hp, g_pre[l], 256)
        xns = _rmsnorm(hs, g_pre[l], SAMPLE_ROWS)
        q, qs = _mm([xn], [xns], w_in, l, 0, ATT_W, scale=HEAD_DIM ** -0.5 * LOG2E, out_dtype=BF16, name="proj_q")
        kst, ks = _mm([xn], [xns], w_in, l, ATT_W, ATT_W, name="proj_k", stack="create" if l == 0 else kst)
        vst, vs = _mm([xn], [xns], w_in, l, 2 * ATT_W, ATT_W, name="proj_v", stack="create" if l == 0 else vst)
        u4, u4s = _mm([xn], [xns], w_in, l, 3 * ATT_W, ATT_W + 3 * CONV_C, name="proj_rest")

        kn_l.append(heads(ks))
        vn_l.append(heads(vs))
        if l == DEPTH - 1:
            ya, new_k, new_v = _attn_prompt(q, kst, vst, u4, trow, l, batch, seq,
                                            caches=(cache_k, cache_v, jnp.stack(kn_l), jnp.stack(vn_l)))
        else:
            (ya,) = _attn_prompt(q, kst, vst, u4, trow, l, batch, seq)
        yb, cstate = _conv_prompt(u4, conv_w[l], conv_b[l], ln_g[l], ln_b[l], batch, seq)

        yas = _attn_sample(heads(qs.astype(F32)), kn_l[l], vn_l[l], heads(u4s), cache_k, cache_v, tab_s, tab_new, l)
        ybs, cstate_s = _conv_sample(u4s, state_conv[l], conv_w[l], conv_b[l], ln_g[l], ln_b[l])

        y, ys = _mm([ya, yb], [pad(yas).astype(BF16), ybs.astype(BF16)], w_out, l, 0, d, name="proj_out")
        h1, h1b, pn = _post(hp, y, p_all, l, g_post[l], wple_bf[l], 128)
        h1s, h1bs, pns = _post(hs, ys, ps_all, l, g_post[l], wple_bf[l], SAMPLE_ROWS)
        hp, hs = _gate_mm(h1b, h1bs, w_pg, b_pg, l, h1, h1s, pn, pns, wple_bf[l], g_ple[l])
        cp_l.append(cstate)
        cs_l.append(cstate_s)

    kv_shape = (DEPTH, batch, seq, N_HEADS, HEAD_DIM)
    return (hp.reshape(batch, seq, d), hs[:nb].reshape(nb, 1, d),
            kst.reshape(kv_shape), vst.reshape(kv_shape), jnp.stack(cp_l),
            new_k, new_v, jnp.stack(cs_l))
```

```python
import functools
import math

import jax
import jax.numpy as jnp
from jax import lax
from jax.experimental import pallas as pl
from jax.experimental.pallas import tpu as pltpu

D_MODEL = 4096
HEAD_DIM = 128
ATT_W = 2048
N_HEADS = 16
CONV_C = 2048
CONV_K = 31
PLE_DIM = 256
N_BUCKETS = 32
MAX_DISTANCE = 2048
WIN_MAX = 2048
EPS = 1e-6
NEG = -1e30
LOG2E = math.log2(math.e)
DEPTH = 2

F32 = jnp.float32
BF16 = jnp.bfloat16

VMEM_LIMIT = 62 * 1024 * 1024
SAMPLE_ROWS = 16
LANE = 128
QBLK = 256
ATT_CHUNK = 512
ATT_SBUF = 2
TAB_W = WIN_MAX + QBLK
TAB_P = TAB_W + QBLK


def _sigmoid(x):
    return 0.5 * jnp.tanh(0.5 * x) + 0.5


def _cparams(sem):
    return pltpu.CompilerParams(dimension_semantics=sem, vmem_limit_bytes=VMEM_LIMIT)


def _rmsnorm_kernel(x_ref, g_ref, o_ref):
    x = x_ref[...]
    r = x * lax.rsqrt(jnp.mean(x * x, axis=-1, keepdims=True) + EPS)
    o_ref[...] = (r * g_ref[...]).astype(o_ref.dtype)


def _rmsnorm(x, g, tm):
    m, d = x.shape
    return pl.pallas_call(
        _rmsnorm_kernel,
        grid=(m // tm,),
        in_specs=[pl.BlockSpec((tm, d), lambda i: (i, 0)), pl.BlockSpec((1, d), lambda i: (0, 0))],
        out_specs=pl.BlockSpec((tm, d), lambda i: (i, 0)),
        out_shape=jax.ShapeDtypeStruct((m, d), BF16),
        compiler_params=_cparams(("parallel",)),
        name="rmsnorm",
    )(x, g.reshape(1, d))


MM_TM = 2048
MM_TN = 256
MM_RB = 1024


def _sample_spec(ms, tn):
    return pl.BlockSpec((None, ms, tn), lambda i, j: (i, 0, j))


def _mm_kernel(*refs, nparts, scale, stack):
    xs = refs[:nparts]
    ss = refs[nparts:2 * nparts]
    ws = refs[2 * nparts:3 * nparts]
    no = 3 * nparts + (1 if stack == "update" else 0)
    o_ref, os_ref = refs[no:no + 2]
    wbs = refs[no + 2:]
    for w_ref, wb_ref in zip(ws, wbs):
        wb_ref[...] = w_ref[...].astype(BF16)

    def product(parts, rows):
        acc = None
        for x_ref, wb_ref in zip(parts, wbs):
            d = jnp.dot(x_ref[rows, :], wb_ref[...], preferred_element_type=F32)
            acc = d if acc is None else acc + d
        return acc if scale == 1.0 else acc * scale

    tm = xs[0].shape[0]
    for r in range(0, tm, min(tm, MM_RB)):
        rows = slice(r, r + min(tm, MM_RB))
        res = product(xs, rows).astype(o_ref.dtype)
        if stack == "create":
            o_ref[0, rows, :] = res
            o_ref[1, rows, :] = jnp.zeros(res.shape, res.dtype)
        else:
            o_ref[rows, :] = res

    @pl.when(pl.program_id(0) == 0)
    def _():
        os_ref[...] = product(ss, slice(None)).astype(os_ref.dtype)

    @pl.when(pl.program_id(0) > 0)
    def _():
        os_ref[...] = jnp.zeros(os_ref.shape, os_ref.dtype)


def _mm(x_parts, s_parts, w, layer, col0, ncols, *, scale=1.0, out_dtype=F32, tm=MM_TM, tn=MM_TN, name="mm",
        stack=None):
    nparts = len(x_parts)
    m = x_parts[0].shape[0]
    ms = s_parts[0].shape[0]
    kparts = [x.shape[1] for x in x_parts]
    assert sum(kparts) == w.shape[1] and col0 % tn == 0 and ncols % tn == 0 and m % tm == 0
    jb0 = col0 // tn
    koff = [sum(kparts[:p]) for p in range(nparts)]
    in_specs = (
        [pl.BlockSpec((tm, kp), lambda i, j: (i, 0)) for kp in kparts]
        + [pl.BlockSpec((ms, kp), lambda i, j: (0, 0)) for kp in kparts]
        + [pl.BlockSpec((None, kp, tn), functools.partial(lambda i, j, kb: (layer, kb, jb0 + j), kb=ko // kp))
           for kp, ko in zip(kparts, koff)]
    )
    args = [*x_parts, *s_parts, *([w] * nparts)]
    aliases = {}
    if stack is None:
        mode = None
        big_spec = pl.BlockSpec((tm, tn), lambda i, j: (i, j))
        big_shape = jax.ShapeDtypeStruct((m, ncols), out_dtype)
    elif isinstance(stack, str):
        assert stack == "create" and layer == 0 and DEPTH == 2
        mode = "create"
        big_spec = pl.BlockSpec((DEPTH, tm, tn), lambda i, j: (0, i, j))
        big_shape = jax.ShapeDtypeStruct((DEPTH, m, ncols), out_dtype)
    else:
        assert stack.shape == (DEPTH, m, ncols) and stack.dtype == out_dtype
        mode = "update"
        in_specs.append(pl.BlockSpec(memory_space=pl.ANY))
        aliases = {len(args): 0}
        args.append(stack)
        big_spec = pl.BlockSpec((None, tm, tn), lambda i, j: (layer, i, j))
        big_shape = jax.ShapeDtypeStruct((DEPTH, m, ncols), out_dtype)
    out, out_s = pl.pallas_call(
        functools.partial(_mm_kernel, nparts=nparts, scale=scale, stack=mode),
        grid=(m // tm, ncols // tn),
        in_specs=in_specs,
        out_specs=[big_spec, _sample_spec(ms, tn)],
        out_shape=[big_shape, jax.ShapeDtypeStruct((m // tm, ms, ncols), out_dtype)],
        scratch_shapes=[pltpu.VMEM((kp, tn), BF16) for kp in kparts],
        input_output_aliases=aliases,
        compiler_params=_cparams(("arbitrary", "arbitrary")),
        name=name,
    )(*args)
    return out, out_s[0]


def _gate_update(x_ref, wb_ref, b_ref, h_ref, pn_ref, wple_ref, gple_ref, rows):
    acc = jnp.dot(x_ref[rows, :], wb_ref[...], preferred_element_type=F32)
    pen = jnp.dot(pn_ref[rows, :], wple_ref[...], preferred_element_type=F32) * gple_ref[...]
    return h_ref[rows, :] + jax.nn.sigmoid(acc + b_ref[...]) * pen


def _gate_kernel(x_ref, s_ref, w_ref, b_ref, h_ref, hs_ref, pn_ref, pns_ref, wple_ref, gple_ref,
                 o_ref, os_ref, wb_ref):
    wb_ref[...] = w_ref[...].astype(BF16)
    tm = x_ref.shape[0]
    for r in range(0, tm, min(tm, MM_RB)):
        rows = slice(r, r + min(tm, MM_RB))
        o_ref[rows, :] = _gate_update(x_ref, wb_ref, b_ref, h_ref, pn_ref, wple_ref, gple_ref, rows)

    @pl.when(pl.program_id(0) == 0)
    def _():
        os_ref[...] = _gate_update(s_ref, wb_ref, b_ref, hs_ref, pns_ref, wple_ref, gple_ref, slice(None))

    @pl.when(pl.program_id(0) > 0)
    def _():
        os_ref[...] = jnp.zeros(os_ref.shape, os_ref.dtype)


def _gate_mm(x, s, w, b, layer, h, hs, pn, pns, w_ple_bf, g_ple, *, tm=MM_TM, tn=MM_TN):
    m, k = x.shape
    ms = s.shape[0]
    n = w.shape[2]
    big = pl.BlockSpec((tm, tn), lambda i, j: (i, j))
    small = pl.BlockSpec((ms, tn), lambda i, j: (0, j))
    out, out_s = pl.pallas_call(
        _gate_kernel,
        grid=(m // tm, n // tn),
        in_specs=[pl.BlockSpec((tm, k), lambda i, j: (i, 0)), pl.BlockSpec((ms, k), lambda i, j: (0, 0)),
                  pl.BlockSpec((None, k, tn), lambda i, j: (layer, 0, j)),
                  pl.BlockSpec((None, 1, tn), lambda i, j: (layer, 0, j)),
                  big, small,
                  pl.BlockSpec((tm, PLE_DIM), lambda i, j: (i, 0)),
                  pl.BlockSpec((ms, PLE_DIM), lambda i, j: (0, 0)),
                  pl.BlockSpec((PLE_DIM, tn), lambda i, j: (0, j)),
                  pl.BlockSpec((1, tn), lambda i, j: (0, j))],
        out_specs=[big, _sample_spec(ms, tn)],
        out_shape=[jax.ShapeDtypeStruct((m, n), F32), jax.ShapeDtypeStruct((m // tm, ms, n), F32)],
        scratch_shapes=[pltpu.VMEM((k, tn), BF16)],
        compiler_params=_cparams(("arbitrary", "arbitrary")),
        name="gate_mm",
    )(x, s, w, b.reshape(DEPTH, 1, n), h, hs, pn, pns, w_ple_bf, g_ple.reshape(1, n))
    return out, out_s[0]


def _post_kernel(h_ref, y_ref, p_ref, gpost_ref, wple_ref, h1_ref, h1b_ref, pn_ref):
    y = y_ref[...]
    r = y * lax.rsqrt(jnp.mean(y * y, axis=-1, keepdims=True) + EPS) * gpost_ref[...]
    h1 = h_ref[...] + r
    h1_ref[...] = h1
    h1b_ref[...] = h1.astype(BF16)
    p = p_ref[...]
    pe = jnp.dot(p.astype(BF16), wple_ref[...], preferred_element_type=F32)
    rs = lax.rsqrt(jnp.mean(pe * pe, axis=-1, keepdims=True) + EPS)
    pn_ref[...] = (p * rs).astype(BF16)


def _post(h, y, p, layer, g_post, w_ple_bf, tm):
    m, d = h.shape
    row = pl.BlockSpec((tm, d), lambda i: (i, 0))
    vec = pl.BlockSpec((1, d), lambda i: (0, 0))
    return pl.pallas_call(
        _post_kernel,
        grid=(m // tm,),
        in_specs=[row, row, pl.BlockSpec((None, tm, PLE_DIM), lambda i: (layer, i, 0)), vec,
                  pl.BlockSpec((PLE_DIM, d), lambda i: (0, 0))],
        out_specs=[row, row, pl.BlockSpec((tm, PLE_DIM), lambda i: (i, 0))],
        out_shape=[jax.ShapeDtypeStruct((m, d), F32), jax.ShapeDtypeStruct((m, d), BF16),
                   jax.ShapeDtypeStruct((m, PLE_DIM), BF16)],
        compiler_params=_cparams(("parallel",)),
        name="post_norm",
    )(h, y, p, g_post.reshape(1, d), w_ple_bf)


def _shift_window(c_ref, nxt_ref, new_ref, o_ref, cache_chunks):
    step = pl.program_id(0) * pl.num_programs(1) + pl.program_id(1)
    is_last = step % cache_chunks == cache_chunks - 1
    t = c_ref.shape[0]
    o_ref[0:t - 1] = c_ref[1:t]
    o_ref[t - 1] = jnp.where(is_last, new_ref[...], nxt_ref[0])


def _shift_specs(cache, new, inner, steps):
    depth, nb, win = cache.shape[:3]
    assert (depth * nb * win) % steps == 0
    rows = depth * nb * win // steps
    assert win % rows == 0
    cpw = win // rows

    def where(a, b):
        s = a * inner + b
        return s // (nb * cpw), (s // cpw) % nb, s % cpw

    tail = cache.shape[3:]
    zeros = (0,) * len(tail)
    chunk = pl.BlockSpec((None, None, rows, *tail), lambda a, b: (*where(a, b), *zeros))
    nxt = pl.BlockSpec((None, None, 1, *tail),
                       lambda a, b: (*where(a, b)[:2], jnp.minimum((where(a, b)[2] + 1) * rows, win - 1), *zeros))
    newspec = pl.BlockSpec((None, None, *tail), lambda a, b: (*where(a, b)[:2], *zeros))
    return [chunk, nxt, newspec], [cache, cache, new], chunk, cpw


def _attn_kernel(*refs, seq, cache_chunks):
    q_ref, k_ref, v_ref, g_ref, trow_ref = refs[:5]
    bid = pl.program_id(1)
    if cache_chunks:
        ck_ref, ckn_ref, kn_ref, o_ref, ok_ref, kb_ref, vb_ref, s_ref, tab_ref = refs[5:]
        _shift_window(ck_ref, ckn_ref, kn_ref, ok_ref, cache_chunks)
    else:
        o_ref, kb_ref, vb_ref, s_ref, tab_ref = refs[5:]

    @pl.when(bid == 0)
    def _():
        tab_ref[...] = pltpu.roll(jnp.broadcast_to(trow_ref[...], (QBLK, TAB_P)), 0, 1, stride=1, stride_axis=0)

    kb_ref[...] = k_ref[...].astype(BF16)
    vb_ref[...] = v_ref[...].astype(BF16)
    nblk = seq // QBLK
    for i in range(nblk):
        q = q_ref[i * QBLK:(i + 1) * QBLK, :]
        nkeys = (i + 1) * QBLK
        tab0 = (nblk - i) * QBLK
        chunks = [(c, min(ATT_CHUNK, nkeys - c)) for c in range(0, nkeys, ATT_CHUNK)]
        mx = jnp.full((QBLK, LANE), NEG, F32)
        for c, w in chunks:
            s = lax.dot_general(q, kb_ref[c:c + w, :], (((1,), (1,)), ((), ())), preferred_element_type=F32)
            s = s + tab_ref[:, tab0 + c:tab0 + c + w]
            s_ref[i % ATT_SBUF, :, c:c + w] = s
            for t in range(0, w, LANE):
                mx = jnp.maximum(mx, s[:, t:t + LANE])
        m = jnp.max(mx, axis=-1, keepdims=True)
        ls = jnp.zeros((QBLK, LANE), F32)
        acc = jnp.zeros((QBLK, HEAD_DIM), F32)
        for c, w in chunks:
            p = jnp.exp2(s_ref[i % ATT_SBUF, :, c:c + w] - m)
            for t in range(0, w, LANE):
                ls = ls + p[:, t:t + LANE]
            acc = acc + jnp.dot(p.astype(BF16), vb_ref[c:c + w, :], preferred_element_type=F32)
        l = jnp.sum(ls, axis=-1, keepdims=True)
        g = g_ref[i * QBLK:(i + 1) * QBLK, :]
        o = acc * (1.0 / l)
        o_ref[i * QBLK:(i + 1) * QBLK, :] = (o * (g * jax.nn.sigmoid(g))).astype(o_ref.dtype)


def _attn_prompt(q, kst, vst, u4, trow, layer, batch, seq, cache=None):
    m = q.shape[0]
    blk = lambda h, b: (b, h)
    kv = pl.BlockSpec((None, seq, HEAD_DIM), lambda h, b: (layer, b, h))
    args = [q, kst, vst, u4, trow]
    in_specs = [pl.BlockSpec((seq, HEAD_DIM), blk), kv, kv, pl.BlockSpec((seq, HEAD_DIM), blk),
                pl.BlockSpec((None, 1, TAB_P), lambda h, b: (h, 0, 0))]
    out_specs = [pl.BlockSpec((seq, HEAD_DIM), blk)]
    out_shape = [jax.ShapeDtypeStruct((m, ATT_W), BF16)]
    cpw = 0
    if cache is not None:
        specs, cargs, ospec, cpw = _shift_specs(*cache, inner=batch, steps=N_HEADS * batch)
        args += cargs
        in_specs += specs
        out_specs.append(ospec)
        out_shape.append(jax.ShapeDtypeStruct(cache[0].shape, cache[0].dtype))
    return pl.pallas_call(
        functools.partial(_attn_kernel, seq=seq, cache_chunks=cpw),
        grid=(N_HEADS, batch),
        in_specs=in_specs,
        out_specs=out_specs,
        out_shape=out_shape,
        scratch_shapes=[pltpu.VMEM((seq, HEAD_DIM), BF16), pltpu.VMEM((seq, HEAD_DIM), BF16),
                        pltpu.VMEM((ATT_SBUF, QBLK, seq), F32), pltpu.VMEM((QBLK, TAB_P), F32)],
        compiler_params=_cparams(("arbitrary", "arbitrary")),
        name="attn_prompt",
    )(*args)


CONV_TB = 256
CONV_HIST = 32
CONV_RC = 32
NCH = CONV_C // LANE


def _conv_kernel(*refs, cache_chunks):
    ga_ref, gb_ref, gate_ref, w_ref, cb_ref, lg_ref, lb_ref = refs[:7]
    if cache_chunks:
        cv_ref, cvn_ref, vn_ref, yb_ref, st_ref, ov_ref, ucat_ref, y_ref = refs[7:]
        _shift_window(cv_ref, cvn_ref, vn_ref, ov_ref, cache_chunks)
    else:
        yb_ref, st_ref, ucat_ref, y_ref = refs[7:]
    t = pl.program_id(1)
    tb = CONV_TB

    @pl.when(t == 0)
    def _():
        ucat_ref[:, 0:CONV_HIST, :] = jnp.zeros((NCH, CONV_HIST, LANE), F32)

    @pl.when(t > 0)
    def _():
        ucat_ref[:, 0:CONV_HIST, :] = ucat_ref[:, tb:tb + CONV_HIST, :]

    for c in range(NCH):
        sl = slice(c * LANE, (c + 1) * LANE)
        ga = ga_ref[:, sl]
        ucat_ref[c, CONV_HIST:CONV_HIST + tb, :] = ga * _sigmoid(gb_ref[:, sl])

    off = CONV_HIST - (CONV_K - 1)

    def chunk(c, carry):
        wc = w_ref[c]
        bias = cb_ref[c]
        for r in range(0, tb, CONV_RC):
            acc = jnp.broadcast_to(bias, (CONV_RC, LANE))
            for k in range(CONV_K):
                acc = acc + wc[k:k + 1, :] * ucat_ref[c, r + off + k:r + off + k + CONV_RC, :]
            y_ref[c, r:r + CONV_RC, :] = acc
        return carry

    lax.fori_loop(0, NCH, chunk, 0)

    ssum = jnp.zeros((tb, LANE), F32)
    for c in range(NCH):
        ssum = ssum + y_ref[c]
    mu = jnp.sum(ssum, axis=-1, keepdims=True) * (1.0 / CONV_C)
    sq = jnp.zeros((tb, LANE), F32)
    for c in range(NCH):
        d = y_ref[c] - mu
        sq = sq + d * d
    rstd = lax.rsqrt(jnp.sum(sq, axis=-1, keepdims=True) * (1.0 / CONV_C) + EPS)
    for c in range(NCH):
        sl = slice(c * LANE, (c + 1) * LANE)
        z = (y_ref[c] - mu) * rstd * lg_ref[c] + lb_ref[c]
        g = gate_ref[:, sl]
        yb_ref[:, sl] = (z * _sigmoid(z) * (g * _sigmoid(g))).astype(yb_ref.dtype)

    @pl.when(t == pl.num_programs(1) - 1)
    def _():
        for c in range(NCH):
            st_ref[:, c * LANE:(c + 1) * LANE] = ucat_ref[c, CONV_HIST + tb - (CONV_K - 1):CONV_HIST + tb, :]


def _chunked(v):
    return v.reshape(v.shape[0], NCH, LANE).transpose(1, 0, 2)


def _conv_prompt(u4, conv_w, conv_b, ln_g, ln_b, batch, seq, cache=None):
    m = u4.shape[0]
    nt = seq // CONV_TB
    wpad = jnp.zeros((CONV_HIST, CONV_C), F32).at[:CONV_K].set(conv_w)
    col = lambda j: pl.BlockSpec((CONV_TB, CONV_C), lambda b, t, j=j: (b * nt + t, j))
    vec = pl.BlockSpec((NCH, 1, LANE), lambda b, t: (0, 0, 0))
    args = [u4, u4, u4, _chunked(wpad), _chunked(conv_b.reshape(1, -1)), _chunked(ln_g.reshape(1, -1)),
            _chunked(ln_b.reshape(1, -1))]
    in_specs = [col(1), col(2), col(3), pl.BlockSpec((NCH, CONV_HIST, LANE), lambda b, t: (0, 0, 0)), vec, vec, vec]
    out_specs = [pl.BlockSpec((CONV_TB, CONV_C), lambda b, t: (b * nt + t, 0)),
                 pl.BlockSpec((None, CONV_K - 1, CONV_C), lambda b, t: (b, 0, 0))]
    out_shape = [jax.ShapeDtypeStruct((m, CONV_C), BF16), jax.ShapeDtypeStruct((batch, CONV_K - 1, CONV_C), F32)]
    cpw = 0
    if cache is not None:
        specs, cargs, ospec, cpw = _shift_specs(*cache, inner=nt, steps=batch * nt)
        args += cargs
        in_specs += specs
        out_specs.append(ospec)
        out_shape.append(jax.ShapeDtypeStruct(cache[0].shape, cache[0].dtype))
    return pl.pallas_call(
        functools.partial(_conv_kernel, cache_chunks=cpw),
        grid=(batch, nt),
        in_specs=in_specs,
        out_specs=out_specs,
        out_shape=out_shape,
        scratch_shapes=[pltpu.VMEM((NCH, CONV_HIST + CONV_TB, LANE), F32), pltpu.VMEM((NCH, CONV_TB, LANE), F32)],
        compiler_params=_cparams(("arbitrary", "arbitrary")),
        name="conv_prompt",
    )(*args)


def _conv_s_kernel(u4_ref, st_ref, w_ref, cb_ref, lg_ref, lb_ref, yb_ref, so_ref, *, nb):
    ga = u4_ref[:, CONV_C:2 * CONV_C]
    u = ga * jax.nn.sigmoid(u4_ref[:, 2 * CONV_C:3 * CONV_C])
    gate = u4_ref[:, 3 * CONV_C:4 * CONV_C]
    yb_ref[...] = jnp.zeros(yb_ref.shape, yb_ref.dtype)
    for b in range(nb):
        st = st_ref[b]
        un = u[b:b + 1, :]
        y = (jnp.sum(st * w_ref[0:CONV_K - 1, :], axis=0, keepdims=True)
             + un * w_ref[CONV_K - 1:CONV_K, :] + cb_ref[...])
        mu = jnp.mean(y, axis=-1, keepdims=True)
        d = y - mu
        z = d * lax.rsqrt(jnp.mean(d * d, axis=-1, keepdims=True) + EPS) * lg_ref[...] + lb_ref[...]
        g = gate[b:b + 1, :]
        yb_ref[b:b + 1, :] = (z * jax.nn.sigmoid(z) * (g * jax.nn.sigmoid(g))).astype(yb_ref.dtype)
        so_ref[b, 0:CONV_K - 2, :] = st[1:CONV_K - 1, :]
        so_ref[b, CONV_K - 2:CONV_K - 1, :] = un


def _conv_sample(u4s, state, conv_w, conv_b, ln_g, ln_b):
    nb = state.shape[0]
    return pl.pallas_call(
        functools.partial(_conv_s_kernel, nb=nb),
        out_shape=[jax.ShapeDtypeStruct((SAMPLE_ROWS, CONV_C), F32),
                   jax.ShapeDtypeStruct(state.shape, F32)],
        compiler_params=pltpu.CompilerParams(vmem_limit_bytes=VMEM_LIMIT),
        name="conv_sample",
    )(u4s, state, conv_w, conv_b.reshape(1, -1), ln_g.reshape(1, -1), ln_b.reshape(1, -1))


def _attn_s_kernel(q_ref, kn_ref, vn_ref, g_ref, k1_ref, k2_ref, k3_ref, v1_ref, v2_ref, v3_ref,
                   tab_ref, tnew_ref, o_ref):
    q = q_ref[...]
    s = [jnp.sum(kr[...] * q[None], axis=-1, keepdims=True) + tab_ref[p]
         for p, kr in enumerate((k1_ref, k2_ref, k3_ref))]
    s_new = jnp.sum(q * kn_ref[...], axis=-1, keepdims=True) + tnew_ref[...]
    m = s_new
    for sp in s:
        m = jnp.maximum(m, jnp.max(sp, axis=0))
    p_new = jnp.exp2(s_new - m)
    l = p_new
    acc = p_new * vn_ref[...]
    for sp, vr in zip(s, (v1_ref, v2_ref, v3_ref)):
        p = jnp.exp2(sp - m[None])
        l = l + jnp.sum(p, axis=0)
        acc = acc + jnp.sum(p * vr[...], axis=0)
    g = g_ref[...]
    o_ref[...] = acc * (1.0 / l) * (g * jax.nn.sigmoid(g))


def _attn_sample(qs, kn, vn, gs, cache_k, cache_v, tab_s, tab_new, layer):
    nb = qs.shape[0]
    rows = 128
    row = pl.BlockSpec((None, N_HEADS, HEAD_DIM), lambda b: (b, 0, 0))
    c1 = lambda c: c.reshape(DEPTH, nb, WIN_MAX // rows, rows, N_HEADS, HEAD_DIM)
    c2 = lambda c: c.reshape(DEPTH, nb, WIN_MAX // 4, 4, N_HEADS, HEAD_DIM)
    c3 = lambda c: c.reshape(DEPTH, nb, WIN_MAX // 16, 16, N_HEADS, HEAD_DIM)
    s1 = pl.BlockSpec((None, None, None, rows, N_HEADS, HEAD_DIM), lambda b: (layer, b, WIN_MAX // rows - 1, 0, 0, 0))
    s2 = pl.BlockSpec((None, None, rows, None, N_HEADS, HEAD_DIM), lambda b: (layer, b, 3, 0, 0, 0))
    s3 = pl.BlockSpec((None, None, rows, None, N_HEADS, HEAD_DIM), lambda b: (layer, b, 0, 0, 0, 0))
    return pl.pallas_call(
        _attn_s_kernel,
        grid=(nb,),
        in_specs=[row, row, row, row, s1, s2, s3, s1, s2, s3,
                  pl.BlockSpec((3, rows, N_HEADS, 1), lambda b: (0, 0, 0, 0)),
                  pl.BlockSpec((N_HEADS, 1), lambda b: (0, 0))],
        out_specs=row,
        out_shape=jax.ShapeDtypeStruct((nb, N_HEADS, HEAD_DIM), F32),
        compiler_params=_cparams(("parallel",)),
        name="attn_sample",
    )(qs, kn, vn, gs, c1(cache_k), c2(cache_k), c3(cache_k), c1(cache_v), c2(cache_v), c3(cache_v),
      tab_s, tab_new)


def _rel_bucket(dist):
    max_exact = N_BUCKETS // 2
    df = jnp.maximum(dist, 1).astype(F32)
    large = max_exact + (jnp.log(df / max_exact) / math.log(MAX_DISTANCE / max_exact)
                         * (N_BUCKETS - max_exact)).astype(jnp.int32)
    large = jnp.minimum(large, N_BUCKETS - 1)
    return jnp.where(dist < max_exact, dist, large)


def _bias_tables(rel_bias):
    dist = jnp.arange(WIN_MAX + 1, dtype=jnp.int32)
    bias = rel_bias[_rel_bucket(dist)].astype(F32).T
    mult = ((dist <= 128).astype(F32) + ((dist % 4 == 0) & (dist <= 512)).astype(F32)
            + (dist % 16 == 0).astype(F32))
    t1d = jnp.where(mult > 0, (bias + jnp.log(jnp.maximum(mult, 1.0))) * LOG2E, NEG)
    trow = jnp.full((N_HEADS, 1, TAB_P), NEG, F32).at[:, 0, :WIN_MAX + 1].set(t1d[:, ::-1])
    j = jnp.arange(128, dtype=jnp.int32)
    d_rows = jnp.stack([128 - j, 512 - 4 * j, 2048 - 16 * j])
    tab_s = (bias[:, d_rows] * LOG2E).transpose(1, 2, 0)[..., None]
    tab_new = ((bias[:, 0] + math.log(3.0)) * LOG2E).reshape(N_HEADS, 1)
    return trow, tab_s, tab_new


def kernel(x_prompt, x_sample, cache_k, cache_v, state_conv, p_prompt, p_sample, rel_bias, g_pre, w_in, conv_w,
           conv_b, ln_g, ln_b, w_out, g_post, w_ple, g_ple, w_pg, b_pg):
    batch, seq, d = x_prompt.shape
    nb = x_sample.shape[0]
    m = batch * seq
    pad = lambda a: jnp.zeros((SAMPLE_ROWS, a.size // nb), a.dtype).at[:nb].set(a.reshape(nb, -1))
    heads = lambda a: a[:nb, :ATT_W].reshape(nb, N_HEADS, HEAD_DIM)

    hp = x_prompt.reshape(m, d)
    hs = pad(x_sample)
    trow, tab_s, tab_new = _bias_tables(rel_bias)
    p_all = p_prompt.reshape(DEPTH, m, PLE_DIM)
    ps_all = jnp.stack([pad(p_sample[l]) for l in range(DEPTH)])
    wple_bf = w_ple.astype(BF16)
    kst = vst = new_k = new_v = None
    cp_l, cs_l, kn_l, vn_l = [], [], [], []
    for l in range(DEPTH):
        xn = _rmsnorm(hp, g_pre[l], 256)
        xns = _rmsnorm(hs, g_pre[l], SAMPLE_ROWS)
        q, qs = _mm([xn], [xns], w_in, l, 0, ATT_W, scale=HEAD_DIM ** -0.5 * LOG2E, out_dtype=BF16, name="proj_q")
        kst, ks = _mm([xn], [xns], w_in, l, ATT_W, ATT_W, name="proj_k", stack="create" if l == 0 else kst)
        vst, vs = _mm([xn], [xns], w_in, l, 2 * ATT_W, ATT_W, name="proj_v", stack="create" if l == 0 else vst)
        u4, u4s = _mm([xn], [xns], w_in, l, 3 * ATT_W, ATT_W + 3 * CONV_C, name="proj_rest")

        kn_l.append(heads(ks))
        vn_l.append(heads(vs))
        if l == DEPTH - 1:
            ya, new_k = _attn_prompt(q, kst, vst, u4, trow, l, batch, seq, cache=(cache_k, jnp.stack(kn_l)))
            yb, cstate, new_v = _conv_prompt(u4, conv_w[l], conv_b[l], ln_g[l], ln_b[l], batch, seq,
                                             cache=(cache_v, jnp.stack(vn_l)))
        else:
            (ya,) = _attn_prompt(q, kst, vst, u4, trow, l, batch, seq)
            yb, cstate = _conv_prompt(u4, conv_w[l], conv_b[l], ln_g[l], ln_b[l], batch, seq)

        yas = _attn_sample(heads(qs.astype(F32)), kn_l[l], vn_l[l], heads(u4s), cache_k, cache_v, tab_s, tab_new, l)
        ybs, cstate_s = _conv_sample(u4s, state_conv[l], conv_w[l], conv_b[l], ln_g[l], ln_b[l])

        y, ys = _mm([ya, yb], [pad(yas).astype(BF16), ybs.astype(BF16)], w_out, l, 0, d, name="proj_out")
        h1, h1b, pn = _post(hp, y, p_all, l, g_post[l], wple_bf[l], 128)
        h1s, h1bs, pns = _post(hs, ys, ps_all, l, g_post[l], wple_bf[l], SAMPLE_ROWS)
        hp, hs = _gate_mm(h1b, h1bs, w_pg, b_pg, l, h1, h1s, pn, pns, wple_bf[l], g_ple[l])
        cp_l.append(cstate)
        cs_l.append(cstate_s)

    kv_shape = (DEPTH, batch, seq, N_HEADS, HEAD_DIM)
    return (hp.reshape(batch, seq, d), hs[:nb].reshape(nb, 1, d),
            kst.reshape(kv_shape), vst.reshape(kv_shape), jnp.stack(cp_l),
            new_k, new_v, jnp.stack(cs_l))
```

```python
import functools
import math

import jax
import jax.numpy as jnp
from jax import lax
from jax.experimental import pallas as pl
from jax.experimental.pallas import tpu as pltpu

D_MODEL = 4096
HEAD_DIM = 128
ATT_W = 2048
N_HEADS = 16
CONV_C = 2048
CONV_K = 31
PLE_DIM = 256
N_BUCKETS = 32
MAX_DISTANCE = 2048
WIN_MAX = 2048
EPS = 1e-6
NEG = -1e30
LOG2E = math.log2(math.e)
DEPTH = 2

F32 = jnp.float32
BF16 = jnp.bfloat16

VMEM_LIMIT = 62 * 1024 * 1024
SAMPLE_ROWS = 16
LANE = 128
QBLK = 256
ATT_CHUNK = 512
ATT_SBUF = 2
TAB_W = WIN_MAX + QBLK
TAB_P = TAB_W + QBLK


def _sigmoid(x):
    return 0.5 * jnp.tanh(0.5 * x) + 0.5


def _cparams(sem):
    return pltpu.CompilerParams(dimension_semantics=sem, vmem_limit_bytes=VMEM_LIMIT)


def _rmsnorm_kernel(x_ref, g_ref, o_ref):
    x = x_ref[...]
    r = x * lax.rsqrt(jnp.mean(x * x, axis=-1, keepdims=True) + EPS)
    o_ref[...] = (r * g_ref[...]).astype(o_ref.dtype)


def _rmsnorm(x, g, tm):
    m, d = x.shape
    return pl.pallas_call(
        _rmsnorm_kernel,
        grid=(m // tm,),
        in_specs=[pl.BlockSpec((tm, d), lambda i: (i, 0)), pl.BlockSpec((1, d), lambda i: (0, 0))],
        out_specs=pl.BlockSpec((tm, d), lambda i: (i, 0)),
        out_shape=jax.ShapeDtypeStruct((m, d), BF16),
        compiler_params=_cparams(("parallel",)),
        name="rmsnorm",
    )(x, g.reshape(1, d))


MM_TM = 2048
MM_TN = 256
MM_RB = 1024


def _sample_spec(ms, tn):
    return pl.BlockSpec((None, ms, tn), lambda i, j: (i, 0, j))


def _mm_kernel(*refs, nparts, scale, stack, cache_chunks):
    xs = refs[:nparts]
    ss = refs[nparts:2 * nparts]
    ws = refs[2 * nparts:3 * nparts]
    no = 3 * nparts + (1 if stack == "update" else 0)
    shift_refs = None
    if cache_chunks:
        shift_refs = (*refs[no:no + 3], refs[no + 5])
        no += 3
    o_ref, os_ref = refs[no:no + 2]
    wbs = refs[no + (3 if cache_chunks else 2):]
    for w_ref, wb_ref in zip(ws, wbs):
        wb_ref[...] = w_ref[...].astype(BF16)

    def product(parts, rows):
        acc = None
        for x_ref, wb_ref in zip(parts, wbs):
            d = jnp.dot(x_ref[rows, :], wb_ref[...], preferred_element_type=F32)
            acc = d if acc is None else acc + d
        return acc if scale == 1.0 else acc * scale

    tm = xs[0].shape[0]
    for r in range(0, tm, min(tm, MM_RB)):
        rows = slice(r, r + min(tm, MM_RB))
        res = product(xs, rows).astype(o_ref.dtype)
        if stack == "create":
            o_ref[0, rows, :] = res
            o_ref[1, rows, :] = jnp.zeros(res.shape, res.dtype)
        else:
            o_ref[rows, :] = res
    if shift_refs is not None:
        _shift_window(*shift_refs, cache_chunks)

    @pl.when(pl.program_id(0) == 0)
    def _():
        os_ref[...] = product(ss, slice(None)).astype(os_ref.dtype)

    @pl.when(pl.program_id(0) > 0)
    def _():
        os_ref[...] = jnp.zeros(os_ref.shape, os_ref.dtype)


def _mm(x_parts, s_parts, w, layer, col0, ncols, *, scale=1.0, out_dtype=F32, tm=MM_TM, tn=MM_TN, name="mm",
        stack=None, cache=None):
    nparts = len(x_parts)
    m = x_parts[0].shape[0]
    ms = s_parts[0].shape[0]
    kparts = [x.shape[1] for x in x_parts]
    assert sum(kparts) == w.shape[1] and col0 % tn == 0 and ncols % tn == 0 and m % tm == 0
    jb0 = col0 // tn
    koff = [sum(kparts[:p]) for p in range(nparts)]
    in_specs = (
        [pl.BlockSpec((tm, kp), lambda i, j: (i, 0)) for kp in kparts]
        + [pl.BlockSpec((ms, kp), lambda i, j: (0, 0)) for kp in kparts]
        + [pl.BlockSpec((None, kp, tn), functools.partial(lambda i, j, kb: (layer, kb, jb0 + j), kb=ko // kp))
           for kp, ko in zip(kparts, koff)]
    )
    args = [*x_parts, *s_parts, *([w] * nparts)]
    aliases = {}
    if stack is None:
        mode = None
        big_spec = pl.BlockSpec((tm, tn), lambda i, j: (i, j))
        big_shape = jax.ShapeDtypeStruct((m, ncols), out_dtype)
    elif isinstance(stack, str):
        assert stack == "create" and layer == 0 and DEPTH == 2
        mode = "create"
        big_spec = pl.BlockSpec((DEPTH, tm, tn), lambda i, j: (0, i, j))
        big_shape = jax.ShapeDtypeStruct((DEPTH, m, ncols), out_dtype)
    else:
        assert stack.shape == (DEPTH, m, ncols) and stack.dtype == out_dtype
        mode = "update"
        in_specs.append(pl.BlockSpec(memory_space=pl.ANY))
        aliases = {len(args): 0}
        args.append(stack)
        big_spec = pl.BlockSpec((None, tm, tn), lambda i, j: (layer, i, j))
        big_shape = jax.ShapeDtypeStruct((DEPTH, m, ncols), out_dtype)
    out_specs = [big_spec, _sample_spec(ms, tn)]
    out_shape = [big_shape, jax.ShapeDtypeStruct((m // tm, ms, ncols), out_dtype)]
    cpw = 0
    if cache is not None:
        specs, cargs, ospec, cpw = _shift_specs(*cache, inner=ncols // tn, steps=(m // tm) * (ncols // tn))
        args += cargs
        in_specs += specs
        out_specs.append(ospec)
        out_shape.append(jax.ShapeDtypeStruct(cache[0].shape, cache[0].dtype))
    out, out_s, *shifted = pl.pallas_call(
        functools.partial(_mm_kernel, nparts=nparts, scale=scale, stack=mode, cache_chunks=cpw),
        grid=(m // tm, ncols // tn),
        in_specs=in_specs,
        out_specs=out_specs,
        out_shape=out_shape,
        scratch_shapes=[pltpu.VMEM((kp, tn), BF16) for kp in kparts],
        input_output_aliases=aliases,
        compiler_params=_cparams(("arbitrary", "arbitrary")),
        name=name,
    )(*args)
    return (out, out_s[0], *shifted)


def _gate_update(x_ref, wb_ref, b_ref, gpost_ref, wple_ref, gple_ref, h_ref, y_ref, rs_ref, pn_ref, rows):
    acc = jnp.dot(x_ref[rows, :], wb_ref[...], preferred_element_type=F32)
    pen = jnp.dot(pn_ref[rows, :], wple_ref[...], preferred_element_type=F32) * gple_ref[...]
    rs = jnp.concatenate([rs_ref[rows, :]] * (acc.shape[1] // LANE), axis=1)
    h1 = h_ref[rows, :] + y_ref[rows, :] * rs * gpost_ref[...]
    return h1 + jax.nn.sigmoid(acc + b_ref[...]) * pen


def _gate_kernel(x_ref, s_ref, w_ref, b_ref, gpost_ref, wple_ref, gple_ref, h_ref, y_ref, rs_ref, pn_ref,
                 hs_ref, ys_ref, rss_ref, pns_ref, o_ref, os_ref, wb_ref):
    wb_ref[...] = w_ref[...].astype(BF16)
    shared = (wb_ref, b_ref, gpost_ref, wple_ref, gple_ref)
    tm = x_ref.shape[0]
    for r in range(0, tm, min(tm, MM_RB)):
        rows = slice(r, r + min(tm, MM_RB))
        o_ref[rows, :] = _gate_update(x_ref, *shared, h_ref, y_ref, rs_ref, pn_ref, rows)

    @pl.when(pl.program_id(0) == 0)
    def _():
        os_ref[...] = _gate_update(s_ref, *shared, hs_ref, ys_ref, rss_ref, pns_ref, slice(None))

    @pl.when(pl.program_id(0) > 0)
    def _():
        os_ref[...] = jnp.zeros(os_ref.shape, os_ref.dtype)


def _gate_mm(x, s, w, b, layer, g_post, w_ple_bf, g_ple, prompt, sample, *, tm=MM_TM, tn=MM_TN):
    m, k = x.shape
    ms = s.shape[0]
    n = w.shape[2]
    vec = pl.BlockSpec((1, tn), lambda i, j: (0, j))

    def group(rows, row_block):
        tile = pl.BlockSpec((rows, tn), lambda i, j: (row_block(i), j))
        return [tile, tile, pl.BlockSpec((rows, LANE), lambda i, j: (row_block(i), 0)),
                pl.BlockSpec((rows, PLE_DIM), lambda i, j: (row_block(i), 0))]

    out, out_s = pl.pallas_call(
        _gate_kernel,
        grid=(m // tm, n // tn),
        in_specs=[pl.BlockSpec((tm, k), lambda i, j: (i, 0)), pl.BlockSpec((ms, k), lambda i, j: (0, 0)),
                  pl.BlockSpec((None, k, tn), lambda i, j: (layer, 0, j)),
                  pl.BlockSpec((None, 1, tn), lambda i, j: (layer, 0, j)),
                  vec, pl.BlockSpec((PLE_DIM, tn), lambda i, j: (0, j)), vec,
                  *group(tm, lambda i: i), *group(ms, lambda i: 0)],
        out_specs=[pl.BlockSpec((tm, tn), lambda i, j: (i, j)), _sample_spec(ms, tn)],
        out_shape=[jax.ShapeDtypeStruct((m, n), F32), jax.ShapeDtypeStruct((m // tm, ms, n), F32)],
        scratch_shapes=[pltpu.VMEM((k, tn), BF16)],
        compiler_params=_cparams(("arbitrary", "arbitrary")),
        name="gate_mm",
    )(x, s, w, b.reshape(DEPTH, 1, n), g_post.reshape(1, n), w_ple_bf, g_ple.reshape(1, n), *prompt, *sample)
    return out, out_s[0]


def _post_kernel(h_ref, y_ref, p_ref, gpost_ref, wple_ref, h1b_ref, rs_ref, pn_ref):
    y = y_ref[...]
    rs_y = lax.rsqrt(jnp.mean(y * y, axis=-1, keepdims=True) + EPS)
    h1b_ref[...] = (h_ref[...] + y * rs_y * gpost_ref[...]).astype(BF16)
    rs_ref[...] = jnp.broadcast_to(rs_y, rs_ref.shape)
    p = p_ref[...]
    pe = jnp.dot(p.astype(BF16), wple_ref[...], preferred_element_type=F32)
    rs = lax.rsqrt(jnp.mean(pe * pe, axis=-1, keepdims=True) + EPS)
    pn_ref[...] = (p * rs).astype(BF16)


def _post(h, y, p, layer, g_post, w_ple_bf, tm):
    m, d = h.shape
    row = pl.BlockSpec((tm, d), lambda i: (i, 0))
    vec = pl.BlockSpec((1, d), lambda i: (0, 0))
    return pl.pallas_call(
        _post_kernel,
        grid=(m // tm,),
        in_specs=[row, row, pl.BlockSpec((None, tm, PLE_DIM), lambda i: (layer, i, 0)), vec,
                  pl.BlockSpec((PLE_DIM, d), lambda i: (0, 0))],
        out_specs=[row, pl.BlockSpec((tm, LANE), lambda i: (i, 0)), pl.BlockSpec((tm, PLE_DIM), lambda i: (i, 0))],
        out_shape=[jax.ShapeDtypeStruct((m, d), BF16), jax.ShapeDtypeStruct((m, LANE), F32),
                   jax.ShapeDtypeStruct((m, PLE_DIM), BF16)],
        compiler_params=_cparams(("parallel",)),
        name="post_norm",
    )(h, y, p, g_post.reshape(1, d), w_ple_bf)


def _shift_window(c_ref, nxt_ref, new_ref, o_ref, cache_chunks):
    step = pl.program_id(0) * pl.num_programs(1) + pl.program_id(1)
    is_last = step % cache_chunks == cache_chunks - 1
    t = c_ref.shape[0]
    o_ref[0:t - 1] = c_ref[1:t]
    o_ref[t - 1] = jnp.where(is_last, new_ref[...], nxt_ref[0])


def _shift_specs(cache, new, inner, steps):
    depth, nb, win = cache.shape[:3]
    assert (depth * nb * win) % steps == 0
    rows = depth * nb * win // steps
    assert win % rows == 0
    cpw = win // rows

    def where(a, b):
        s = a * inner + b
        return s // (nb * cpw), (s // cpw) % nb, s % cpw

    tail = cache.shape[3:]
    zeros = (0,) * len(tail)
    chunk = pl.BlockSpec((None, None, rows, *tail), lambda a, b: (*where(a, b), *zeros))
    nxt = pl.BlockSpec((None, None, 1, *tail),
                       lambda a, b: (*where(a, b)[:2], jnp.minimum((where(a, b)[2] + 1) * rows, win - 1), *zeros))
    newspec = pl.BlockSpec((None, None, *tail), lambda a, b: (*where(a, b)[:2], *zeros))
    return [chunk, nxt, newspec], [cache, cache, new], chunk, cpw


def _attn_kernel(*refs, seq, cache_chunks):
    q_ref, k_ref, v_ref, g_ref, trow_ref = refs[:5]
    bid = pl.program_id(1)
    if cache_chunks:
        ck_ref, ckn_ref, kn_ref, o_ref, ok_ref, kb_ref, vb_ref, s_ref, tab_ref = refs[5:]
        _shift_window(ck_ref, ckn_ref, kn_ref, ok_ref, cache_chunks)
    else:
        o_ref, kb_ref, vb_ref, s_ref, tab_ref = refs[5:]

    @pl.when(bid == 0)
    def _():
        tab_ref[...] = pltpu.roll(jnp.broadcast_to(trow_ref[...], (QBLK, TAB_P)), 0, 1, stride=1, stride_axis=0)

    kb_ref[...] = k_ref[...].astype(BF16)
    vb_ref[...] = v_ref[...].astype(BF16)
    nblk = seq // QBLK
    for i in range(nblk):
        q = q_ref[i * QBLK:(i + 1) * QBLK, :]
        nkeys = (i + 1) * QBLK
        tab0 = (nblk - i) * QBLK
        chunks = [(c, min(ATT_CHUNK, nkeys - c)) for c in range(0, nkeys, ATT_CHUNK)]
        mx = jnp.full((QBLK, LANE), NEG, F32)
        for c, w in chunks:
            s = lax.dot_general(q, kb_ref[c:c + w, :], (((1,), (1,)), ((), ())), preferred_element_type=F32)
            s = s + tab_ref[:, tab0 + c:tab0 + c + w]
            s_ref[i % ATT_SBUF, :, c:c + w] = s
            for t in range(0, w, LANE):
                mx = jnp.maximum(mx, s[:, t:t + LANE])
        m = jnp.max(mx, axis=-1, keepdims=True)
        ls = jnp.zeros((QBLK, LANE), F32)
        acc = jnp.zeros((QBLK, HEAD_DIM), F32)
        for c, w in chunks:
            p = jnp.exp2(s_ref[i % ATT_SBUF, :, c:c + w] - m)
            for t in range(0, w, LANE):
                ls = ls + p[:, t:t + LANE]
            acc = acc + jnp.dot(p.astype(BF16), vb_ref[c:c + w, :], preferred_element_type=F32)
        l = jnp.sum(ls, axis=-1, keepdims=True)
        g = g_ref[i * QBLK:(i + 1) * QBLK, :]
        o = acc * (1.0 / l)
        o_ref[i * QBLK:(i + 1) * QBLK, :] = (o * (g * jax.nn.sigmoid(g))).astype(o_ref.dtype)


def _attn_prompt(q, kst, vst, u4, trow, layer, batch, seq, cache=None):
    m = q.shape[0]
    blk = lambda h, b: (b, h)
    kv = pl.BlockSpec((None, seq, HEAD_DIM), lambda h, b: (layer, b, h))
    args = [q, kst, vst, u4, trow]
    in_specs = [pl.BlockSpec((seq, HEAD_DIM), blk), kv, kv, pl.BlockSpec((seq, HEAD_DIM), blk),
                pl.BlockSpec((None, 1, TAB_P), lambda h, b: (h, 0, 0))]
    out_specs = [pl.BlockSpec((seq, HEAD_DIM), blk)]
    out_shape = [jax.ShapeDtypeStruct((m, ATT_W), BF16)]
    cpw = 0
    if cache is not None:
        specs, cargs, ospec, cpw = _shift_specs(*cache, inner=batch, steps=N_HEADS * batch)
        args += cargs
        in_specs += specs
        out_specs.append(ospec)
        out_shape.append(jax.ShapeDtypeStruct(cache[0].shape, cache[0].dtype))
    return pl.pallas_call(
        functools.partial(_attn_kernel, seq=seq, cache_chunks=cpw),
        grid=(N_HEADS, batch),
        in_specs=in_specs,
        out_specs=out_specs,
        out_shape=out_shape,
        scratch_shapes=[pltpu.VMEM((seq, HEAD_DIM), BF16), pltpu.VMEM((seq, HEAD_DIM), BF16),
                        pltpu.VMEM((ATT_SBUF, QBLK, seq), F32), pltpu.VMEM((QBLK, TAB_P), F32)],
        compiler_params=_cparams(("arbitrary", "arbitrary")),
        name="attn_prompt",
    )(*args)


CONV_TB = 256
CONV_HIST = 32
CONV_RC = 32
NCH = CONV_C // LANE


def _conv_kernel(ga_ref, gb_ref, gate_ref, w_ref, cb_ref, lg_ref, lb_ref, yb_ref, st_ref, ucat_ref, y_ref):
    t = pl.program_id(1)
    tb = CONV_TB

    @pl.when(t == 0)
    def _():
        ucat_ref[:, 0:CONV_HIST, :] = jnp.zeros((NCH, CONV_HIST, LANE), F32)

    @pl.when(t > 0)
    def _():
        ucat_ref[:, 0:CONV_HIST, :] = ucat_ref[:, tb:tb + CONV_HIST, :]

    for c in range(NCH):
        sl = slice(c * LANE, (c + 1) * LANE)
        ga = ga_ref[:, sl]
        ucat_ref[c, CONV_HIST:CONV_HIST + tb, :] = ga * _sigmoid(gb_ref[:, sl])

    off = CONV_HIST - (CONV_K - 1)

    def chunk(c, carry):
        wc = w_ref[c]
        bias = cb_ref[c]
        for r in range(0, tb, CONV_RC):
            acc = jnp.broadcast_to(bias, (CONV_RC, LANE))
            for k in range(CONV_K):
                acc = acc + wc[k:k + 1, :] * ucat_ref[c, r + off + k:r + off + k + CONV_RC, :]
            y_ref[c, r:r + CONV_RC, :] = acc
        return carry

    lax.fori_loop(0, NCH, chunk, 0)

    ssum = jnp.zeros((tb, LANE), F32)
    for c in range(NCH):
        ssum = ssum + y_ref[c]
    mu = jnp.sum(ssum, axis=-1, keepdims=True) * (1.0 / CONV_C)
    sq = jnp.zeros((tb, LANE), F32)
    for c in range(NCH):
        d = y_ref[c] - mu
        sq = sq + d * d
    rstd = lax.rsqrt(jnp.sum(sq, axis=-1, keepdims=True) * (1.0 / CONV_C) + EPS)
    for c in range(NCH):
        sl = slice(c * LANE, (c + 1) * LANE)
        z = (y_ref[c] - mu) * rstd * lg_ref[c] + lb_ref[c]
        g = gate_ref[:, sl]
        yb_ref[:, sl] = (z * _sigmoid(z) * (g * _sigmoid(g))).astype(yb_ref.dtype)

    @pl.when(t == pl.num_programs(1) - 1)
    def _():
        for c in range(NCH):
            st_ref[:, c * LANE:(c + 1) * LANE] = ucat_ref[c, CONV_HIST + tb - (CONV_K - 1):CONV_HIST + tb, :]


def _chunked(v):
    return v.reshape(v.shape[0], NCH, LANE).transpose(1, 0, 2)


def _conv_prompt(u4, conv_w, conv_b, ln_g, ln_b, batch, seq):
    m = u4.shape[0]
    nt = seq // CONV_TB
    wpad = jnp.zeros((CONV_HIST, CONV_C), F32).at[:CONV_K].set(conv_w)
    col = lambda j: pl.BlockSpec((CONV_TB, CONV_C), lambda b, t, j=j: (b * nt + t, j))
    vec = pl.BlockSpec((NCH, 1, LANE), lambda b, t: (0, 0, 0))
    return pl.pallas_call(
        _conv_kernel,
        grid=(batch, nt),
        in_specs=[col(1), col(2), col(3), pl.BlockSpec((NCH, CONV_HIST, LANE), lambda b, t: (0, 0, 0)), vec, vec, vec],
        out_specs=[pl.BlockSpec((CONV_TB, CONV_C), lambda b, t: (b * nt + t, 0)),
                   pl.BlockSpec((None, CONV_K - 1, CONV_C), lambda b, t: (b, 0, 0))],
        out_shape=[jax.ShapeDtypeStruct((m, CONV_C), BF16),
                   jax.ShapeDtypeStruct((batch, CONV_K - 1, CONV_C), F32)],
        scratch_shapes=[pltpu.VMEM((NCH, CONV_HIST + CONV_TB, LANE), F32), pltpu.VMEM((NCH, CONV_TB, LANE), F32)],
        compiler_params=_cparams(("parallel", "arbitrary")),
        name="conv_prompt",
    )(u4, u4, u4, _chunked(wpad), _chunked(conv_b.reshape(1, -1)), _chunked(ln_g.reshape(1, -1)),
      _chunked(ln_b.reshape(1, -1)))


def _conv_s_kernel(u4_ref, st_ref, w_ref, cb_ref, lg_ref, lb_ref, yb_ref, so_ref, *, nb):
    ga = u4_ref[:, CONV_C:2 * CONV_C]
    u = ga * jax.nn.sigmoid(u4_ref[:, 2 * CONV_C:3 * CONV_C])
    gate = u4_ref[:, 3 * CONV_C:4 * CONV_C]
    yb_ref[...] = jnp.zeros(yb_ref.shape, yb_ref.dtype)
    for b in range(nb):
        st = st_ref[b]
        un = u[b:b + 1, :]
        y = (jnp.sum(st * w_ref[0:CONV_K - 1, :], axis=0, keepdims=True)
             + un * w_ref[CONV_K - 1:CONV_K, :] + cb_ref[...])
        mu = jnp.mean(y, axis=-1, keepdims=True)
        d = y - mu
        z = d * lax.rsqrt(jnp.mean(d * d, axis=-1, keepdims=True) + EPS) * lg_ref[...] + lb_ref[...]
        g = gate[b:b + 1, :]
        yb_ref[b:b + 1, :] = (z * jax.nn.sigmoid(z) * (g * jax.nn.sigmoid(g))).astype(yb_ref.dtype)
        so_ref[b, 0:CONV_K - 2, :] = st[1:CONV_K - 1, :]
        so_ref[b, CONV_K - 2:CONV_K - 1, :] = un


def _conv_sample(u4s, state, conv_w, conv_b, ln_g, ln_b):
    nb = state.shape[0]
    return pl.pallas_call(
        functools.partial(_conv_s_kernel, nb=nb),
        out_shape=[jax.ShapeDtypeStruct((SAMPLE_ROWS, CONV_C), F32),
                   jax.ShapeDtypeStruct(state.shape, F32)],
        compiler_params=pltpu.CompilerParams(vmem_limit_bytes=VMEM_LIMIT),
        name="conv_sample",
    )(u4s, state, conv_w, conv_b.reshape(1, -1), ln_g.reshape(1, -1), ln_b.reshape(1, -1))


def _attn_s_kernel(q_ref, kn_ref, vn_ref, g_ref, k1_ref, k2_ref, k3_ref, v1_ref, v2_ref, v3_ref,
                   tab_ref, tnew_ref, o_ref):
    q = q_ref[...]
    s = [jnp.sum(kr[...] * q[None], axis=-1, keepdims=True) + tab_ref[p]
         for p, kr in enumerate((k1_ref, k2_ref, k3_ref))]
    s_new = jnp.sum(q * kn_ref[...], axis=-1, keepdims=True) + tnew_ref[...]
    m = s_new
    for sp in s:
        m = jnp.maximum(m, jnp.max(sp, axis=0))
    p_new = jnp.exp2(s_new - m)
    l = p_new
    acc = p_new * vn_ref[...]
    for sp, vr in zip(s, (v1_ref, v2_ref, v3_ref)):
        p = jnp.exp2(sp - m[None])
        l = l + jnp.sum(p, axis=0)
        acc = acc + jnp.sum(p * vr[...], axis=0)
    g = g_ref[...]
    o_ref[...] = acc * (1.0 / l) * (g * jax.nn.sigmoid(g))


def _attn_sample(qs, kn, vn, gs, cache_k, cache_v, tab_s, tab_new, layer):
    nb = qs.shape[0]
    rows = 128
    row = pl.BlockSpec((None, N_HEADS, HEAD_DIM), lambda b: (b, 0, 0))
    c1 = lambda c: c.reshape(DEPTH, nb, WIN_MAX // rows, rows, N_HEADS, HEAD_DIM)
    c2 = lambda c: c.reshape(DEPTH, nb, WIN_MAX // 4, 4, N_HEADS, HEAD_DIM)
    c3 = lambda c: c.reshape(DEPTH, nb, WIN_MAX // 16, 16, N_HEADS, HEAD_DIM)
    s1 = pl.BlockSpec((None, None, None, rows, N_HEADS, HEAD_DIM), lambda b: (layer, b, WIN_MAX // rows - 1, 0, 0, 0))
    s2 = pl.BlockSpec((None, None, rows, None, N_HEADS, HEAD_DIM), lambda b: (layer, b, 3, 0, 0, 0))
    s3 = pl.BlockSpec((None, None, rows, None, N_HEADS, HEAD_DIM), lambda b: (layer, b, 0, 0, 0, 0))
    return pl.pallas_call(
        _attn_s_kernel,
        grid=(nb,),
        in_specs=[row, row, row, row, s1, s2, s3, s1, s2, s3,
                  pl.BlockSpec((3, rows, N_HEADS, 1), lambda b: (0, 0, 0, 0)),
                  pl.BlockSpec((N_HEADS, 1), lambda b: (0, 0))],
        out_specs=row,
        out_shape=jax.ShapeDtypeStruct((nb, N_HEADS, HEAD_DIM), F32),
        compiler_params=_cparams(("parallel",)),
        name="attn_sample",
    )(qs, kn, vn, gs, c1(cache_k), c2(cache_k), c3(cache_k), c1(cache_v), c2(cache_v), c3(cache_v),
      tab_s, tab_new)


def _rel_bucket(dist):
    max_exact = N_BUCKETS // 2
    df = jnp.maximum(dist, 1).astype(F32)
    large = max_exact + (jnp.log(df / max_exact) / math.log(MAX_DISTANCE / max_exact)
                         * (N_BUCKETS - max_exact)).astype(jnp.int32)
    large = jnp.minimum(large, N_BUCKETS - 1)
    return jnp.where(dist < max_exact, dist, large)


def _bias_tables(rel_bias):
    dist = jnp.arange(WIN_MAX + 1, dtype=jnp.int32)
    bias = rel_bias[_rel_bucket(dist)].astype(F32).T
    mult = ((dist <= 128).astype(F32) + ((dist % 4 == 0) & (dist <= 512)).astype(F32)
            + (dist % 16 == 0).astype(F32))
    t1d = jnp.where(mult > 0, (bias + jnp.log(jnp.maximum(mult, 1.0))) * LOG2E, NEG)
    trow = jnp.full((N_HEADS, 1, TAB_P), NEG, F32).at[:, 0, :WIN_MAX + 1].set(t1d[:, ::-1])
    j = jnp.arange(128, dtype=jnp.int32)
    d_rows = jnp.stack([128 - j, 512 - 4 * j, 2048 - 16 * j])
    tab_s = (bias[:, d_rows] * LOG2E).transpose(1, 2, 0)[..., None]
    tab_new = ((bias[:, 0] + math.log(3.0)) * LOG2E).reshape(N_HEADS, 1)
    return trow, tab_s, tab_new


def kernel(x_prompt, x_sample, cache_k, cache_v, state_conv, p_prompt, p_sample, rel_bias, g_pre, w_in, conv_w,
           conv_b, ln_g, ln_b, w_out, g_post, w_ple, g_ple, w_pg, b_pg):
    batch, seq, d = x_prompt.shape
    nb = x_sample.shape[0]
    m = batch * seq
    pad = lambda a: jnp.zeros((SAMPLE_ROWS, a.size // nb), a.dtype).at[:nb].set(a.reshape(nb, -1))
    heads = lambda a: a[:nb, :ATT_W].reshape(nb, N_HEADS, HEAD_DIM)

    hp = x_prompt.reshape(m, d)
    hs = pad(x_sample)
    trow, tab_s, tab_new = _bias_tables(rel_bias)
    p_all = p_prompt.reshape(DEPTH, m, PLE_DIM)
    ps_all = jnp.stack([pad(p_sample[l]) for l in range(DEPTH)])
    wple_bf = w_ple.astype(BF16)
    kst = vst = new_k = new_v = None
    cp_l, cs_l, kn_l, vn_l = [], [], [], []
    for l in range(DEPTH):
        xn = _rmsnorm(hp, g_pre[l], 256)
        xns = _rmsnorm(hs, g_pre[l], SAMPLE_ROWS)
        q, qs = _mm([xn], [xns], w_in, l, 0, ATT_W, scale=HEAD_DIM ** -0.5 * LOG2E, out_dtype=BF16, name="proj_q")
        kst, ks = _mm([xn], [xns], w_in, l, ATT_W, ATT_W, name="proj_k", stack="create" if l == 0 else kst)
        vst, vs = _mm([xn], [xns], w_in, l, 2 * ATT_W, ATT_W, name="proj_v", stack="create" if l == 0 else vst)
        kn_l.append(heads(ks))
        vn_l.append(heads(vs))
        if l == DEPTH - 1:
            u4, u4s, new_v = _mm([xn], [xns], w_in, l, 3 * ATT_W, ATT_W + 3 * CONV_C, name="proj_rest",
                                 cache=(cache_v, jnp.stack(vn_l)))
            ya, new_k = _attn_prompt(q, kst, vst, u4, trow, l, batch, seq, cache=(cache_k, jnp.stack(kn_l)))
        else:
            u4, u4s = _mm([xn], [xns], w_in, l, 3 * ATT_W, ATT_W + 3 * CONV_C, name="proj_rest")
            (ya,) = _attn_prompt(q, kst, vst, u4, trow, l, batch, seq)
        yb, cstate = _conv_prompt(u4, conv_w[l], conv_b[l], ln_g[l], ln_b[l], batch, seq)

        yas = _attn_sample(heads(qs.astype(F32)), kn_l[l], vn_l[l], heads(u4s), cache_k, cache_v, tab_s, tab_new, l)
        ybs, cstate_s = _conv_sample(u4s, state_conv[l], conv_w[l], conv_b[l], ln_g[l], ln_b[l])

        y, ys = _mm([ya, yb], [pad(yas).astype(BF16), ybs.astype(BF16)], w_out, l, 0, d, name="proj_out")
        h1b, rs, pn = _post(hp, y, p_all, l, g_post[l], wple_bf[l], 128)
        h1bs, rss, pns = _post(hs, ys, ps_all, l, g_post[l], wple_bf[l], SAMPLE_ROWS)
        hp, hs = _gate_mm(h1b, h1bs, w_pg, b_pg, l, g_post[l], wple_bf[l], g_ple[l],
                          (hp, y, rs, pn), (hs, ys, rss, pns))
        cp_l.append(cstate)
        cs_l.append(cstate_s)

    kv_shape = (DEPTH, batch, seq, N_HEADS, HEAD_DIM)
    return (hp.reshape(batch, seq, d), hs[:nb].reshape(nb, 1, d),
            kst.reshape(kv_shape), vst.reshape(kv_shape), jnp.stack(cp_l),
            new_k, new_v, jnp.stack(cs_l))
```

```python
import functools
import math

import jax
import jax.numpy as jnp
from jax import lax
from jax.experimental import pallas as pl
from jax.experimental.pallas import tpu as pltpu

D_MODEL = 4096
HEAD_DIM = 128
ATT_W = 2048
N_HEADS = 16
CONV_C = 2048
CONV_K = 31
PLE_DIM = 256
N_BUCKETS = 32
MAX_DISTANCE = 2048
WIN_MAX = 2048
EPS = 1e-6
NEG = -1e30
LOG2E = math.log2(math.e)
Q_SCALE = HEAD_DIM ** -0.5 * LOG2E
DEPTH = 2

F32 = jnp.float32
BF16 = jnp.bfloat16

VMEM_LIMIT = 62 * 1024 * 1024
SAMPLE_ROWS = 16
LANE = 128
QBLK = 256
ATT_CHUNK = 512
ATT_SBUF = 2
TAB_W = WIN_MAX + QBLK
TAB_P = TAB_W + QBLK


def _sigmoid(x):
    return 0.5 * jnp.tanh(0.5 * x) + 0.5


def _cparams(sem):
    return pltpu.CompilerParams(dimension_semantics=sem, vmem_limit_bytes=VMEM_LIMIT)


def _rmsnorm_kernel(x_ref, g_ref, o_ref):
    x = x_ref[...]
    r = x * lax.rsqrt(jnp.mean(x * x, axis=-1, keepdims=True) + EPS)
    o_ref[...] = (r * g_ref[...]).astype(o_ref.dtype)


def _rmsnorm(x, g, tm):
    m, d = x.shape
    return pl.pallas_call(
        _rmsnorm_kernel,
        grid=(m // tm,),
        in_specs=[pl.BlockSpec((tm, d), lambda i: (i, 0)), pl.BlockSpec((1, d), lambda i: (0, 0))],
        out_specs=pl.BlockSpec((tm, d), lambda i: (i, 0)),
        out_shape=jax.ShapeDtypeStruct((m, d), BF16),
        compiler_params=_cparams(("parallel",)),
        name="rmsnorm",
    )(x, g.reshape(1, d))


MM_TM = 2048
MM_TN = 256
MM_RB = 256


def _sample_spec(ms, tn):
    return pl.BlockSpec((None, ms, tn), lambda i, j: (i, 0, j))


def _mm_kernel(*refs, nparts, cache_chunks):
    xs = refs[:nparts]
    ss = refs[nparts:2 * nparts]
    ws = refs[2 * nparts:3 * nparts]
    no = 3 * nparts
    shift_refs = None
    if cache_chunks:
        shift_refs = (*refs[no:no + 3], refs[no + 5])
        no += 3
    o_ref, os_ref = refs[no:no + 2]
    wbs = refs[no + (3 if cache_chunks else 2):]
    for w_ref, wb_ref in zip(ws, wbs):
        wb_ref[...] = w_ref[...].astype(BF16)

    def product(parts, rows):
        acc = None
        for x_ref, wb_ref in zip(parts, wbs):
            d = jnp.dot(x_ref[rows, :], wb_ref[...], preferred_element_type=F32)
            acc = d if acc is None else acc + d
        return acc

    tm = xs[0].shape[0]
    for r in range(0, tm, min(tm, MM_RB)):
        rows = slice(r, r + min(tm, MM_RB))
        o_ref[rows, :] = product(xs, rows).astype(o_ref.dtype)
    if shift_refs is not None:
        _shift_window(*shift_refs, cache_chunks)

    @pl.when(pl.program_id(0) == 0)
    def _():
        os_ref[...] = product(ss, slice(None)).astype(os_ref.dtype)

    @pl.when(pl.program_id(0) > 0)
    def _():
        os_ref[...] = jnp.zeros(os_ref.shape, os_ref.dtype)


def _mm(x_parts, s_parts, w, layer, col0, ncols, *, out_dtype=F32, tm=MM_TM, tn=MM_TN, name="mm", cache=None):
    nparts = len(x_parts)
    m = x_parts[0].shape[0]
    ms = s_parts[0].shape[0]
    kparts = [x.shape[1] for x in x_parts]
    assert sum(kparts) == w.shape[1] and col0 % tn == 0 and ncols % tn == 0 and m % tm == 0
    jb0 = col0 // tn
    koff = [sum(kparts[:p]) for p in range(nparts)]
    in_specs = (
        [pl.BlockSpec((tm, kp), lambda i, j: (i, 0)) for kp in kparts]
        + [pl.BlockSpec((ms, kp), lambda i, j: (0, 0)) for kp in kparts]
        + [pl.BlockSpec((None, kp, tn), functools.partial(lambda i, j, kb: (layer, kb, jb0 + j), kb=ko // kp))
           for kp, ko in zip(kparts, koff)]
    )
    args = [*x_parts, *s_parts, *([w] * nparts)]
    out_specs = [pl.BlockSpec((tm, tn), lambda i, j: (i, j)), _sample_spec(ms, tn)]
    out_shape = [jax.ShapeDtypeStruct((m, ncols), out_dtype),
                 jax.ShapeDtypeStruct((m // tm, ms, ncols), out_dtype)]
    cpw = 0
    if cache is not None:
        specs, cargs, ospec, cpw = _shift_specs(*cache, inner=ncols // tn, steps=(m // tm) * (ncols // tn))
        args += cargs
        in_specs += specs
        out_specs.append(ospec)
        out_shape.append(jax.ShapeDtypeStruct(cache[0].shape, cache[0].dtype))
    out, out_s, *shifted = pl.pallas_call(
        functools.partial(_mm_kernel, nparts=nparts, cache_chunks=cpw),
        grid=(m // tm, ncols // tn),
        in_specs=in_specs,
        out_specs=out_specs,
        out_shape=out_shape,
        scratch_shapes=[pltpu.VMEM((kp, tn), BF16) for kp in kparts],
        compiler_params=_cparams(("arbitrary", "arbitrary")),
        name=name,
    )(*args)
    return (out, out_s[0], *shifted)


def _gate_update(x_ref, wb_ref, b_ref, gpost_ref, wple_ref, gple_ref, h_ref, y_ref, rs_ref, pn_ref, rows):
    acc = jnp.dot(x_ref[rows, :], wb_ref[...], preferred_element_type=F32)
    pen = jnp.dot(pn_ref[rows, :], wple_ref[...], preferred_element_type=F32) * gple_ref[...]
    rs = jnp.concatenate([rs_ref[rows, :]] * (acc.shape[1] // LANE), axis=1)
    h1 = h_ref[rows, :] + y_ref[rows, :] * rs * gpost_ref[...]
    return h1 + jax.nn.sigmoid(acc + b_ref[...]) * pen


def _gate_kernel(x_ref, s_ref, w_ref, b_ref, gpost_ref, wple_ref, gple_ref, h_ref, y_ref, rs_ref, pn_ref,
                 hs_ref, ys_ref, rss_ref, pns_ref, o_ref, os_ref, wb_ref):
    wb_ref[...] = w_ref[...].astype(BF16)
    shared = (wb_ref, b_ref, gpost_ref, wple_ref, gple_ref)
    tm = x_ref.shape[0]
    for r in range(0, tm, min(tm, MM_RB)):
        rows = slice(r, r + min(tm, MM_RB))
        o_ref[rows, :] = _gate_update(x_ref, *shared, h_ref, y_ref, rs_ref, pn_ref, rows)

    @pl.when(pl.program_id(0) == 0)
    def _():
        os_ref[...] = _gate_update(s_ref, *shared, hs_ref, ys_ref, rss_ref, pns_ref, slice(None))

    @pl.when(pl.program_id(0) > 0)
    def _():
        os_ref[...] = jnp.zeros(os_ref.shape, os_ref.dtype)


def _gate_mm(x, s, w, b, layer, g_post, w_ple_bf, g_ple, prompt, sample, *, tm=MM_TM, tn=MM_TN):
    m, k = x.shape
    ms = s.shape[0]
    n = w.shape[2]
    vec = pl.BlockSpec((1, tn), lambda i, j: (0, j))

    def group(rows, row_block):
        tile = pl.BlockSpec((rows, tn), lambda i, j: (row_block(i), j))
        return [tile, tile, pl.BlockSpec((rows, LANE), lambda i, j: (row_block(i), 0)),
                pl.BlockSpec((rows, PLE_DIM), lambda i, j: (row_block(i), 0))]

    out, out_s = pl.pallas_call(
        _gate_kernel,
        grid=(m // tm, n // tn),
        in_specs=[pl.BlockSpec((tm, k), lambda i, j: (i, 0)), pl.BlockSpec((ms, k), lambda i, j: (0, 0)),
                  pl.BlockSpec((None, k, tn), lambda i, j: (layer, 0, j)),
                  pl.BlockSpec((None, 1, tn), lambda i, j: (layer, 0, j)),
                  vec, pl.BlockSpec((PLE_DIM, tn), lambda i, j: (0, j)), vec,
                  *group(tm, lambda i: i), *group(ms, lambda i: 0)],
        out_specs=[pl.BlockSpec((tm, tn), lambda i, j: (i, j)), _sample_spec(ms, tn)],
        out_shape=[jax.ShapeDtypeStruct((m, n), F32), jax.ShapeDtypeStruct((m // tm, ms, n), F32)],
        scratch_shapes=[pltpu.VMEM((k, tn), BF16)],
        compiler_params=_cparams(("arbitrary", "arbitrary")),
        name="gate_mm",
    )(x, s, w, b.reshape(DEPTH, 1, n), g_post.reshape(1, n), w_ple_bf, g_ple.reshape(1, n), *prompt, *sample)
    return out, out_s[0]


def _post_kernel(h_ref, y_ref, p_ref, gpost_ref, wple_ref, h1b_ref, rs_ref, pn_ref):
    y = y_ref[...]
    rs_y = lax.rsqrt(jnp.mean(y * y, axis=-1, keepdims=True) + EPS)
    h1b_ref[...] = (h_ref[...] + y * rs_y * gpost_ref[...]).astype(BF16)
    rs_ref[...] = jnp.broadcast_to(rs_y, rs_ref.shape)
    p = p_ref[...]
    pe = jnp.dot(p.astype(BF16), wple_ref[...], preferred_element_type=F32)
    rs = lax.rsqrt(jnp.mean(pe * pe, axis=-1, keepdims=True) + EPS)
    pn_ref[...] = (p * rs).astype(BF16)


def _post(h, y, p, layer, g_post, w_ple_bf, tm):
    m, d = h.shape
    row = pl.BlockSpec((tm, d), lambda i: (i, 0))
    vec = pl.BlockSpec((1, d), lambda i: (0, 0))
    return pl.pallas_call(
        _post_kernel,
        grid=(m // tm,),
        in_specs=[row, row, pl.BlockSpec((None, tm, PLE_DIM), lambda i: (layer, i, 0)), vec,
                  pl.BlockSpec((PLE_DIM, d), lambda i: (0, 0))],
        out_specs=[row, pl.BlockSpec((tm, LANE), lambda i: (i, 0)), pl.BlockSpec((tm, PLE_DIM), lambda i: (i, 0))],
        out_shape=[jax.ShapeDtypeStruct((m, d), BF16), jax.ShapeDtypeStruct((m, LANE), F32),
                   jax.ShapeDtypeStruct((m, PLE_DIM), BF16)],
        compiler_params=_cparams(("parallel",)),
        name="post_norm",
    )(h, y, p, g_post.reshape(1, d), w_ple_bf)


def _shift_window(c_ref, nxt_ref, new_ref, o_ref, cache_chunks):
    step = pl.program_id(0) * pl.num_programs(1) + pl.program_id(1)
    is_last = step % cache_chunks == cache_chunks - 1
    t = c_ref.shape[0]
    o_ref[0:t - 1] = c_ref[1:t]
    o_ref[t - 1] = jnp.where(is_last, new_ref[...], nxt_ref[0])


def _shift_specs(cache, new, inner, steps):
    depth, nb, win = cache.shape[:3]
    assert (depth * nb * win) % steps == 0
    rows = depth * nb * win // steps
    assert win % rows == 0
    cpw = win // rows

    def where(a, b):
        s = a * inner + b
        return s // (nb * cpw), (s // cpw) % nb, s % cpw

    tail = cache.shape[3:]
    zeros = (0,) * len(tail)
    chunk = pl.BlockSpec((None, None, rows, *tail), lambda a, b: (*where(a, b), *zeros))
    nxt = pl.BlockSpec((None, None, 1, *tail),
                       lambda a, b: (*where(a, b)[:2], jnp.minimum((where(a, b)[2] + 1) * rows, win - 1), *zeros))
    newspec = pl.BlockSpec((None, None, *tail), lambda a, b: (*where(a, b)[:2], *zeros))
    return [chunk, nxt, newspec], [cache, cache, new], chunk, cpw


def _attn_kernel(*refs, seq, cache_chunks, first_layer):
    q_ref, k_ref, v_ref, g_ref, trow_ref = refs[:5]
    ni = 5 + (3 if cache_chunks else 0) + (0 if first_layer else 2)
    o_ref, ko_ref, vo_ref = refs[ni:ni + 3]
    kb_ref, vb_ref, s_ref, tab_ref = refs[-4:]
    bid = pl.program_id(1)
    if cache_chunks:
        _shift_window(*refs[5:8], refs[ni + 3], cache_chunks)

    @pl.when(bid == 0)
    def _():
        tab_ref[...] = pltpu.roll(jnp.broadcast_to(trow_ref[...], (QBLK, TAB_P)), 0, 1, stride=1, stride_axis=0)

    k = k_ref[...]
    v = v_ref[...]
    kb_ref[...] = k.astype(BF16)
    vb_ref[...] = v.astype(BF16)
    if first_layer:
        ko_ref[0], vo_ref[0] = k, v
        ko_ref[1] = jnp.zeros(k.shape, k.dtype)
        vo_ref[1] = jnp.zeros(v.shape, v.dtype)
    else:
        ko_ref[...] = k
        vo_ref[...] = v
    nblk = seq // QBLK
    for i in range(nblk):
        q = (q_ref[i * QBLK:(i + 1) * QBLK, :] * Q_SCALE).astype(BF16)
        nkeys = (i + 1) * QBLK
        tab0 = (nblk - i) * QBLK
        chunks = [(c, min(ATT_CHUNK, nkeys - c)) for c in range(0, nkeys, ATT_CHUNK)]
        mx = jnp.full((QBLK, LANE), NEG, F32)
        for c, w in chunks:
            s = lax.dot_general(q, kb_ref[c:c + w, :], (((1,), (1,)), ((), ())), preferred_element_type=F32)
            s = s + tab_ref[:, tab0 + c:tab0 + c + w]
            s_ref[i % ATT_SBUF, :, c:c + w] = s
            for t in range(0, w, LANE):
                mx = jnp.maximum(mx, s[:, t:t + LANE])
        m = jnp.max(mx, axis=-1, keepdims=True)
        ls = jnp.zeros((QBLK, LANE), F32)
        acc = jnp.zeros((QBLK, HEAD_DIM), F32)
        for c, w in chunks:
            p = jnp.exp2(s_ref[i % ATT_SBUF, :, c:c + w] - m)
            for t in range(0, w, LANE):
                ls = ls + p[:, t:t + LANE]
            acc = acc + jnp.dot(p.astype(BF16), vb_ref[c:c + w, :], preferred_element_type=F32)
        l = jnp.sum(ls, axis=-1, keepdims=True)
        g = g_ref[i * QBLK:(i + 1) * QBLK, :]
        o = acc * (1.0 / l)
        o_ref[i * QBLK:(i + 1) * QBLK, :] = (o * (g * jax.nn.sigmoid(g))).astype(o_ref.dtype)


def _attn_prompt(qkv, u4, trow, layer, batch, seq, stacks=None, cache=None):
    m = qkv.shape[0]
    first = stacks is None
    assert first == (layer == 0) and DEPTH == 2
    blk = lambda h, b: (b, h)
    col = lambda part: pl.BlockSpec((seq, HEAD_DIM), lambda h, b: (b, part * N_HEADS + h))
    args = [qkv, qkv, qkv, u4, trow]
    in_specs = [col(0), col(1), col(2), pl.BlockSpec((seq, HEAD_DIM), blk),
                pl.BlockSpec((None, 1, TAB_P), lambda h, b: (h, 0, 0))]
    stack_shape = jax.ShapeDtypeStruct((DEPTH, m, ATT_W), F32)
    if first:
        stack_spec = pl.BlockSpec((DEPTH, seq, HEAD_DIM), lambda h, b: (0, b, h))
    else:
        stack_spec = pl.BlockSpec((None, seq, HEAD_DIM), lambda h, b: (layer, b, h))
    out_specs = [pl.BlockSpec((seq, HEAD_DIM), blk), stack_spec, stack_spec]
    out_shape = [jax.ShapeDtypeStruct((m, ATT_W), BF16), stack_shape, stack_shape]
    cpw = 0
    if cache is not None:
        specs, cargs, ospec, cpw = _shift_specs(*cache, inner=batch, steps=N_HEADS * batch)
        args += cargs
        in_specs += specs
        out_specs.append(ospec)
        out_shape.append(jax.ShapeDtypeStruct(cache[0].shape, cache[0].dtype))
    aliases = {}
    if not first:
        aliases = {len(args): 1, len(args) + 1: 2}
        args += list(stacks)
        in_specs += [pl.BlockSpec(memory_space=pl.ANY)] * 2
    return pl.pallas_call(
        functools.partial(_attn_kernel, seq=seq, cache_chunks=cpw, first_layer=first),
        input_output_aliases=aliases,
        grid=(N_HEADS, batch),
        in_specs=in_specs,
        out_specs=out_specs,
        out_shape=out_shape,
        scratch_shapes=[pltpu.VMEM((seq, HEAD_DIM), BF16), pltpu.VMEM((seq, HEAD_DIM), BF16),
                        pltpu.VMEM((ATT_SBUF, QBLK, seq), F32), pltpu.VMEM((QBLK, TAB_P), F32)],
        compiler_params=_cparams(("arbitrary", "arbitrary")),
        name="attn_prompt",
    )(*args)


CONV_TB = 256
CONV_HIST = 32
CONV_RC = 32
NCH = CONV_C // LANE


def _conv_kernel(ga_ref, gb_ref, gate_ref, w_ref, cb_ref, lg_ref, lb_ref, yb_ref, st_ref, ucat_ref, y_ref):
    t = pl.program_id(1)
    tb = CONV_TB

    @pl.when(t == 0)
    def _():
        ucat_ref[:, 0:CONV_HIST, :] = jnp.zeros((NCH, CONV_HIST, LANE), F32)

    @pl.when(t > 0)
    def _():
        ucat_ref[:, 0:CONV_HIST, :] = ucat_ref[:, tb:tb + CONV_HIST, :]

    for c in range(NCH):
        sl = slice(c * LANE, (c + 1) * LANE)
        ga = ga_ref[:, sl]
        ucat_ref[c, CONV_HIST:CONV_HIST + tb, :] = ga * _sigmoid(gb_ref[:, sl])

    off = CONV_HIST - (CONV_K - 1)

    def chunk(c, carry):
        wc = w_ref[c]
        bias = cb_ref[c]
        for r in range(0, tb, CONV_RC):
            acc = jnp.broadcast_to(bias, (CONV_RC, LANE))
            for k in range(CONV_K):
                acc = acc + wc[k:k + 1, :] * ucat_ref[c, r + off + k:r + off + k + CONV_RC, :]
            y_ref[c, r:r + CONV_RC, :] = acc
        return carry

    lax.fori_loop(0, NCH, chunk, 0)

    ssum = jnp.zeros((tb, LANE), F32)
    for c in range(NCH):
        ssum = ssum + y_ref[c]
    mu = jnp.sum(ssum, axis=-1, keepdims=True) * (1.0 / CONV_C)
    sq = jnp.zeros((tb, LANE), F32)
    for c in range(NCH):
        d = y_ref[c] - mu
        sq = sq + d * d
    rstd = lax.rsqrt(jnp.sum(sq, axis=-1, keepdims=True) * (1.0 / CONV_C) + EPS)
    for c in range(NCH):
        sl = slice(c * LANE, (c + 1) * LANE)
        z = (y_ref[c] - mu) * rstd * lg_ref[c] + lb_ref[c]
        g = gate_ref[:, sl]
        yb_ref[:, sl] = (z * _sigmoid(z) * (g * _sigmoid(g))).astype(yb_ref.dtype)

    @pl.when(t == pl.num_programs(1) - 1)
    def _():
        for c in range(NCH):
            st_ref[:, c * LANE:(c + 1) * LANE] = ucat_ref[c, CONV_HIST + tb - (CONV_K - 1):CONV_HIST + tb, :]


def _chunked(v):
    return v.reshape(v.shape[0], NCH, LANE).transpose(1, 0, 2)


def _conv_prompt(u4, conv_w, conv_b, ln_g, ln_b, batch, seq):
    m = u4.shape[0]
    nt = seq // CONV_TB
    wpad = jnp.zeros((CONV_HIST, CONV_C), F32).at[:CONV_K].set(conv_w)
    col = lambda j: pl.BlockSpec((CONV_TB, CONV_C), lambda b, t, j=j: (b * nt + t, j))
    vec = pl.BlockSpec((NCH, 1, LANE), lambda b, t: (0, 0, 0))
    return pl.pallas_call(
        _conv_kernel,
        grid=(batch, nt),
        in_specs=[col(1), col(2), col(3), pl.BlockSpec((NCH, CONV_HIST, LANE), lambda b, t: (0, 0, 0)), vec, vec, vec],
        out_specs=[pl.BlockSpec((CONV_TB, CONV_C), lambda b, t: (b * nt + t, 0)),
                   pl.BlockSpec((None, CONV_K - 1, CONV_C), lambda b, t: (b, 0, 0))],
        out_shape=[jax.ShapeDtypeStruct((m, CONV_C), BF16),
                   jax.ShapeDtypeStruct((batch, CONV_K - 1, CONV_C), F32)],
        scratch_shapes=[pltpu.VMEM((NCH, CONV_HIST + CONV_TB, LANE), F32), pltpu.VMEM((NCH, CONV_TB, LANE), F32)],
        compiler_params=_cparams(("parallel", "arbitrary")),
        name="conv_prompt",
    )(u4, u4, u4, _chunked(wpad), _chunked(conv_b.reshape(1, -1)), _chunked(ln_g.reshape(1, -1)),
      _chunked(ln_b.reshape(1, -1)))


def _conv_s_kernel(u4_ref, st_ref, w_ref, cb_ref, lg_ref, lb_ref, yb_ref, so_ref, *, nb):
    ga = u4_ref[:, CONV_C:2 * CONV_C]
    u = ga * jax.nn.sigmoid(u4_ref[:, 2 * CONV_C:3 * CONV_C])
    gate = u4_ref[:, 3 * CONV_C:4 * CONV_C]
    yb_ref[...] = jnp.zeros(yb_ref.shape, yb_ref.dtype)
    for b in range(nb):
        st = st_ref[b]
        un = u[b:b + 1, :]
        y = (jnp.sum(st * w_ref[0:CONV_K - 1, :], axis=0, keepdims=True)
             + un * w_ref[CONV_K - 1:CONV_K, :] + cb_ref[...])
        mu = jnp.mean(y, axis=-1, keepdims=True)
        d = y - mu
        z = d * lax.rsqrt(jnp.mean(d * d, axis=-1, keepdims=True) + EPS) * lg_ref[...] + lb_ref[...]
        g = gate[b:b + 1, :]
        yb_ref[b:b + 1, :] = (z * jax.nn.sigmoid(z) * (g * jax.nn.sigmoid(g))).astype(yb_ref.dtype)
        so_ref[b, 0:CONV_K - 2, :] = st[1:CONV_K - 1, :]
        so_ref[b, CONV_K - 2:CONV_K - 1, :] = un


def _conv_sample(u4s, state, conv_w, conv_b, ln_g, ln_b):
    nb = state.shape[0]
    return pl.pallas_call(
        functools.partial(_conv_s_kernel, nb=nb),
        out_shape=[jax.ShapeDtypeStruct((SAMPLE_ROWS, CONV_C), F32),
                   jax.ShapeDtypeStruct(state.shape, F32)],
        compiler_params=pltpu.CompilerParams(vmem_limit_bytes=VMEM_LIMIT),
        name="conv_sample",
    )(u4s, state, conv_w, conv_b.reshape(1, -1), ln_g.reshape(1, -1), ln_b.reshape(1, -1))


def _attn_s_kernel(q_ref, kn_ref, vn_ref, g_ref, k1_ref, k2_ref, k3_ref, v1_ref, v2_ref, v3_ref,
                   tab_ref, tnew_ref, o_ref):
    q = q_ref[...]
    s = [jnp.sum(kr[...] * q[None], axis=-1, keepdims=True) + tab_ref[p]
         for p, kr in enumerate((k1_ref, k2_ref, k3_ref))]
    s_new = jnp.sum(q * kn_ref[...], axis=-1, keepdims=True) + tnew_ref[...]
    m = s_new
    for sp in s:
        m = jnp.maximum(m, jnp.max(sp, axis=0))
    p_new = jnp.exp2(s_new - m)
    l = p_new
    acc = p_new * vn_ref[...]
    for sp, vr in zip(s, (v1_ref, v2_ref, v3_ref)):
        p = jnp.exp2(sp - m[None])
        l = l + jnp.sum(p, axis=0)
        acc = acc + jnp.sum(p * vr[...], axis=0)
    g = g_ref[...]
    o_ref[...] = acc * (1.0 / l) * (g * jax.nn.sigmoid(g))


def _attn_sample(qs, kn, vn, gs, cache_k, cache_v, tab_s, tab_new, layer):
    nb = qs.shape[0]
    rows = 128
    row = pl.BlockSpec((None, N_HEADS, HEAD_DIM), lambda b: (b, 0, 0))
    c1 = lambda c: c.reshape(DEPTH, nb, WIN_MAX // rows, rows, N_HEADS, HEAD_DIM)
    c2 = lambda c: c.reshape(DEPTH, nb, WIN_MAX // 4, 4, N_HEADS, HEAD_DIM)
    c3 = lambda c: c.reshape(DEPTH, nb, WIN_MAX // 16, 16, N_HEADS, HEAD_DIM)
    s1 = pl.BlockSpec((None, None, None, rows, N_HEADS, HEAD_DIM), lambda b: (layer, b, WIN_MAX // rows - 1, 0, 0, 0))
    s2 = pl.BlockSpec((None, None, rows, None, N_HEADS, HEAD_DIM), lambda b: (layer, b, 3, 0, 0, 0))
    s3 = pl.BlockSpec((None, None, rows, None, N_HEADS, HEAD_DIM), lambda b: (layer, b, 0, 0, 0, 0))
    return pl.pallas_call(
        _attn_s_kernel,
        grid=(nb,),
        in_specs=[row, row, row, row, s1, s2, s3, s1, s2, s3,
                  pl.BlockSpec((3, rows, N_HEADS, 1), lambda b: (0, 0, 0, 0)),
                  pl.BlockSpec((N_HEADS, 1), lambda b: (0, 0))],
        out_specs=row,
        out_shape=jax.ShapeDtypeStruct((nb, N_HEADS, HEAD_DIM), F32),
        compiler_params=_cparams(("parallel",)),
        name="attn_sample",
    )(qs, kn, vn, gs, c1(cache_k), c2(cache_k), c3(cache_k), c1(cache_v), c2(cache_v), c3(cache_v),
      tab_s, tab_new)


def _rel_bucket(dist):
    max_exact = N_BUCKETS // 2
    df = jnp.maximum(dist, 1).astype(F32)
    large = max_exact + (jnp.log(df / max_exact) / math.log(MAX_DISTANCE / max_exact)
                         * (N_BUCKETS - max_exact)).astype(jnp.int32)
    large = jnp.minimum(large, N_BUCKETS - 1)
    return jnp.where(dist < max_exact, dist, large)


def _bias_tables(rel_bias):
    dist = jnp.arange(WIN_MAX + 1, dtype=jnp.int32)
    bias = rel_bias[_rel_bucket(dist)].astype(F32).T
    mult = ((dist <= 128).astype(F32) + ((dist % 4 == 0) & (dist <= 512)).astype(F32)
            + (dist % 16 == 0).astype(F32))
    t1d = jnp.where(mult > 0, (bias + jnp.log(jnp.maximum(mult, 1.0))) * LOG2E, NEG)
    trow = jnp.full((N_HEADS, 1, TAB_P), NEG, F32).at[:, 0, :WIN_MAX + 1].set(t1d[:, ::-1])
    j = jnp.arange(128, dtype=jnp.int32)
    d_rows = jnp.stack([128 - j, 512 - 4 * j, 2048 - 16 * j])
    tab_s = (bias[:, d_rows] * LOG2E).transpose(1, 2, 0)[..., None]
    tab_new = ((bias[:, 0] + math.log(3.0)) * LOG2E).reshape(N_HEADS, 1)
    return trow, tab_s, tab_new


def kernel(x_prompt, x_sample, cache_k, cache_v, state_conv, p_prompt, p_sample, rel_bias, g_pre, w_in, conv_w,
           conv_b, ln_g, ln_b, w_out, g_post, w_ple, g_ple, w_pg, b_pg):
    batch, seq, d = x_prompt.shape
    nb = x_sample.shape[0]
    m = batch * seq
    pad = lambda a: jnp.zeros((SAMPLE_ROWS, a.size // nb), a.dtype).at[:nb].set(a.reshape(nb, -1))
    heads = lambda a: a[:nb, :ATT_W].reshape(nb, N_HEADS, HEAD_DIM)

    hp = x_prompt.reshape(m, d)
    hs = pad(x_sample)
    trow, tab_s, tab_new = _bias_tables(rel_bias)
    p_all = p_prompt.reshape(DEPTH, m, PLE_DIM)
    ps_all = jnp.stack([pad(p_sample[l]) for l in range(DEPTH)])
    wple_bf = w_ple.astype(BF16)
    stacks = new_k = new_v = None
    cp_l, cs_l, kn_l, vn_l = [], [], [], []
    for l in range(DEPTH):
        xn = _rmsnorm(hp, g_pre[l], 256)
        xns = _rmsnorm(hs, g_pre[l], SAMPLE_ROWS)
        qkv, qkvs = _mm([xn], [xns], w_in, l, 0, 3 * ATT_W, name="proj_qkv")
        qs = heads(qkvs) * Q_SCALE
        kn_l.append(heads(qkvs[:, ATT_W:]))
        vn_l.append(heads(qkvs[:, 2 * ATT_W:]))
        if l == DEPTH - 1:
            u4, u4s, new_v = _mm([xn], [xns], w_in, l, 3 * ATT_W, ATT_W + 3 * CONV_C, name="proj_rest",
                                 cache=(cache_v, jnp.stack(vn_l)))
            ya, *stacks, new_k = _attn_prompt(qkv, u4, trow, l, batch, seq, stacks=stacks,
                                              cache=(cache_k, jnp.stack(kn_l)))
        else:
            u4, u4s = _mm([xn], [xns], w_in, l, 3 * ATT_W, ATT_W + 3 * CONV_C, name="proj_rest")
            ya, *stacks = _attn_prompt(qkv, u4, trow, l, batch, seq, stacks=stacks)
        yb, cstate = _conv_prompt(u4, conv_w[l], conv_b[l], ln_g[l], ln_b[l], batch, seq)

        yas = _attn_sample(qs, kn_l[l], vn_l[l], heads(u4s), cache_k, cache_v, tab_s, tab_new, l)
        ybs, cstate_s = _conv_sample(u4s, state_conv[l], conv_w[l], conv_b[l], ln_g[l], ln_b[l])

        y, ys = _mm([ya, yb], [pad(yas).astype(BF16), ybs.astype(BF16)], w_out, l, 0, d, name="proj_out")
        h1b, rs, pn = _post(hp, y, p_all, l, g_post[l], wple_bf[l], 128)
        h1bs, rss, pns = _post(hs, ys, ps_all, l, g_post[l], wple_bf[l], SAMPLE_ROWS)
        hp, hs = _gate_mm(h1b, h1bs, w_pg, b_pg, l, g_post[l], wple_bf[l], g_ple[l],
                          (hp, y, rs, pn), (hs, ys, rss, pns))
        cp_l.append(cstate)
        cs_l.append(cstate_s)

    kv_shape = (DEPTH, batch, seq, N_HEADS, HEAD_DIM)
    return (hp.reshape(batch, seq, d), hs[:nb].reshape(nb, 1, d),
            stacks[0].reshape(kv_shape), stacks[1].reshape(kv_shape), jnp.stack(cp_l),
            new_k, new_v, jnp.stack(cs_l))
```

```python
import functools
import math

import jax
import jax.numpy as jnp
from jax import lax
from jax.experimental import pallas as pl
from jax.experimental.pallas import tpu as pltpu

D_MODEL = 4096
HEAD_DIM = 128
ATT_W = 2048
N_HEADS = 16
CONV_C = 2048
CONV_K = 31
PLE_DIM = 256
N_BUCKETS = 32
MAX_DISTANCE = 2048
WIN_MAX = 2048
EPS = 1e-6
NEG = -1e30
LOG2E = math.log2(math.e)
Q_SCALE = HEAD_DIM ** -0.5 * LOG2E
DEPTH = 2

F32 = jnp.float32
BF16 = jnp.bfloat16

VMEM_LIMIT = 62 * 1024 * 1024
SAMPLE_ROWS = 16
LANE = 128
QBLK = 256
ATT_CHUNK = 512
ATT_SBUF = 2
TAB_W = WIN_MAX + QBLK
TAB_P = TAB_W + QBLK


def _sigmoid(x):
    return 0.5 * jnp.tanh(0.5 * x) + 0.5


def _cparams(sem):
    return pltpu.CompilerParams(dimension_semantics=sem, vmem_limit_bytes=VMEM_LIMIT)


def _rmsnorm_kernel(x_ref, g_ref, o_ref):
    x = x_ref[...]
    r = x * lax.rsqrt(jnp.mean(x * x, axis=-1, keepdims=True) + EPS)
    o_ref[...] = (r * g_ref[...]).astype(o_ref.dtype)


def _rmsnorm(x, g, tm):
    m, d = x.shape
    return pl.pallas_call(
        _rmsnorm_kernel,
        grid=(m // tm,),
        in_specs=[pl.BlockSpec((tm, d), lambda i: (i, 0)), pl.BlockSpec((1, d), lambda i: (0, 0))],
        out_specs=pl.BlockSpec((tm, d), lambda i: (i, 0)),
        out_shape=jax.ShapeDtypeStruct((m, d), BF16),
        compiler_params=_cparams(("parallel",)),
        name="rmsnorm",
    )(x, g.reshape(1, d))


MM_TM = 2048
MM_TN = 256
MM_RB = 256


def _sample_spec(ms, tn):
    return pl.BlockSpec((None, ms, tn), lambda i, j: (i, 0, j))


def _mm_kernel(*refs, nparts, cache_chunks):
    xs = refs[:nparts]
    ss = refs[nparts:2 * nparts]
    ws = refs[2 * nparts:3 * nparts]
    no = 3 * nparts
    shift_refs = None
    if cache_chunks:
        shift_refs = (*refs[no:no + 3], refs[no + 5])
        no += 3
    o_ref, os_ref = refs[no:no + 2]
    wbs = refs[no + (3 if cache_chunks else 2):]
    for w_ref, wb_ref in zip(ws, wbs):
        wb_ref[...] = w_ref[...].astype(BF16)

    def product(parts, rows):
        acc = None
        for x_ref, wb_ref in zip(parts, wbs):
            d = jnp.dot(x_ref[rows, :], wb_ref[...], preferred_element_type=F32)
            acc = d if acc is None else acc + d
        return acc

    tm = xs[0].shape[0]
    for r in range(0, tm, min(tm, MM_RB)):
        rows = slice(r, r + min(tm, MM_RB))
        o_ref[rows, :] = product(xs, rows).astype(o_ref.dtype)
    if shift_refs is not None:
        _shift_window(*shift_refs, cache_chunks)

    @pl.when(pl.program_id(0) == 0)
    def _():
        os_ref[...] = product(ss, slice(None)).astype(os_ref.dtype)

    @pl.when(pl.program_id(0) > 0)
    def _():
        os_ref[...] = jnp.zeros(os_ref.shape, os_ref.dtype)


def _mm(x_parts, s_parts, w, layer, col0, ncols, *, out_dtype=F32, tm=MM_TM, tn=MM_TN, name="mm", cache=None):
    nparts = len(x_parts)
    m = x_parts[0].shape[0]
    ms = s_parts[0].shape[0]
    kparts = [x.shape[1] for x in x_parts]
    assert sum(kparts) == w.shape[1] and col0 % tn == 0 and ncols % tn == 0 and m % tm == 0
    jb0 = col0 // tn
    koff = [sum(kparts[:p]) for p in range(nparts)]
    in_specs = (
        [pl.BlockSpec((tm, kp), lambda i, j: (i, 0)) for kp in kparts]
        + [pl.BlockSpec((ms, kp), lambda i, j: (0, 0)) for kp in kparts]
        + [pl.BlockSpec((None, kp, tn), functools.partial(lambda i, j, kb: (layer, kb, jb0 + j), kb=ko // kp))
           for kp, ko in zip(kparts, koff)]
    )
    args = [*x_parts, *s_parts, *([w] * nparts)]
    out_specs = [pl.BlockSpec((tm, tn), lambda i, j: (i, j)), _sample_spec(ms, tn)]
    out_shape = [jax.ShapeDtypeStruct((m, ncols), out_dtype),
                 jax.ShapeDtypeStruct((m // tm, ms, ncols), out_dtype)]
    cpw = 0
    if cache is not None:
        specs, cargs, ospec, cpw = _shift_specs(*cache, inner=ncols // tn, steps=(m // tm) * (ncols // tn))
        args += cargs
        in_specs += specs
        out_specs.append(ospec)
        out_shape.append(jax.ShapeDtypeStruct(cache[0].shape, cache[0].dtype))
    out, out_s, *shifted = pl.pallas_call(
        functools.partial(_mm_kernel, nparts=nparts, cache_chunks=cpw),
        grid=(m // tm, ncols // tn),
        in_specs=in_specs,
        out_specs=out_specs,
        out_shape=out_shape,
        scratch_shapes=[pltpu.VMEM((kp, tn), BF16) for kp in kparts],
        compiler_params=_cparams(("arbitrary", "arbitrary")),
        name=name,
    )(*args)
    return (out, out_s[0], *shifted)


def _gate_update(x_ref, wb_ref, b_ref, gpost_ref, wple_ref, gple_ref, h_ref, y_ref, rs_ref, pn_ref, rows):
    acc = jnp.dot(x_ref[rows, :], wb_ref[...], preferred_element_type=F32)
    pen = jnp.dot(pn_ref[rows, :], wple_ref[...], preferred_element_type=F32) * gple_ref[...]
    rs = jnp.concatenate([rs_ref[rows, :]] * (acc.shape[1] // LANE), axis=1)
    h1 = h_ref[rows, :] + y_ref[rows, :] * rs * gpost_ref[...]
    return h1 + jax.nn.sigmoid(acc + b_ref[...]) * pen


def _gate_kernel(x_ref, s_ref, w_ref, b_ref, gpost_ref, wple_ref, gple_ref, h_ref, y_ref, rs_ref, pn_ref,
                 hs_ref, ys_ref, rss_ref, pns_ref, o_ref, os_ref, wb_ref):
    wb_ref[...] = w_ref[...].astype(BF16)
    shared = (wb_ref, b_ref, gpost_ref, wple_ref, gple_ref)
    tm = x_ref.shape[0]
    for r in range(0, tm, min(tm, MM_RB)):
        rows = slice(r, r + min(tm, MM_RB))
        o_ref[rows, :] = _gate_update(x_ref, *shared, h_ref, y_ref, rs_ref, pn_ref, rows)

    @pl.when(pl.program_id(0) == 0)
    def _():
        os_ref[...] = _gate_update(s_ref, *shared, hs_ref, ys_ref, rss_ref, pns_ref, slice(None))

    @pl.when(pl.program_id(0) > 0)
    def _():
        os_ref[...] = jnp.zeros(os_ref.shape, os_ref.dtype)


def _gate_mm(x, s, w, b, layer, g_post, w_ple_bf, g_ple, prompt, sample, *, tm=MM_TM, tn=MM_TN):
    m, k = x.shape
    ms = s.shape[0]
    n = w.shape[2]
    vec = pl.BlockSpec((1, tn), lambda i, j: (0, j))

    def group(rows, row_block):
        tile = pl.BlockSpec((rows, tn), lambda i, j: (row_block(i), j))
        return [tile, tile, pl.BlockSpec((rows, LANE), lambda i, j: (row_block(i), 0)),
                pl.BlockSpec((rows, PLE_DIM), lambda i, j: (row_block(i), 0))]

    out, out_s = pl.pallas_call(
        _gate_kernel,
        grid=(m // tm, n // tn),
        in_specs=[pl.BlockSpec((tm, k), lambda i, j: (i, 0)), pl.BlockSpec((ms, k), lambda i, j: (0, 0)),
                  pl.BlockSpec((None, k, tn), lambda i, j: (layer, 0, j)),
                  pl.BlockSpec((None, 1, tn), lambda i, j: (layer, 0, j)),
                  vec, pl.BlockSpec((PLE_DIM, tn), lambda i, j: (0, j)), vec,
                  *group(tm, lambda i: i), *group(ms, lambda i: 0)],
        out_specs=[pl.BlockSpec((tm, tn), lambda i, j: (i, j)), _sample_spec(ms, tn)],
        out_shape=[jax.ShapeDtypeStruct((m, n), F32), jax.ShapeDtypeStruct((m // tm, ms, n), F32)],
        scratch_shapes=[pltpu.VMEM((k, tn), BF16)],
        compiler_params=_cparams(("arbitrary", "arbitrary")),
        name="gate_mm",
    )(x, s, w, b.reshape(DEPTH, 1, n), g_post.reshape(1, n), w_ple_bf, g_ple.reshape(1, n), *prompt, *sample)
    return out, out_s[0]


def _post_kernel(h_ref, y_ref, p_ref, gpost_ref, wple_ref, h1b_ref, rs_ref, pn_ref):
    y = y_ref[...]
    rs_y = lax.rsqrt(jnp.mean(y * y, axis=-1, keepdims=True) + EPS)
    h1b_ref[...] = (h_ref[...] + y * rs_y * gpost_ref[...]).astype(BF16)
    rs_ref[...] = jnp.broadcast_to(rs_y, rs_ref.shape)
    p = p_ref[...]
    pe = jnp.dot(p.astype(BF16), wple_ref[...], preferred_element_type=F32)
    rs = lax.rsqrt(jnp.mean(pe * pe, axis=-1, keepdims=True) + EPS)
    pn_ref[...] = (p * rs).astype(BF16)


def _post(h, y, p, layer, g_post, w_ple_bf, tm):
    m, d = h.shape
    row = pl.BlockSpec((tm, d), lambda i: (i, 0))
    vec = pl.BlockSpec((1, d), lambda i: (0, 0))
    return pl.pallas_call(
        _post_kernel,
        grid=(m // tm,),
        in_specs=[row, row, pl.BlockSpec((None, tm, PLE_DIM), lambda i: (layer, i, 0)), vec,
                  pl.BlockSpec((PLE_DIM, d), lambda i: (0, 0))],
        out_specs=[row, pl.BlockSpec((tm, LANE), lambda i: (i, 0)), pl.BlockSpec((tm, PLE_DIM), lambda i: (i, 0))],
        out_shape=[jax.ShapeDtypeStruct((m, d), BF16), jax.ShapeDtypeStruct((m, LANE), F32),
                   jax.ShapeDtypeStruct((m, PLE_DIM), BF16)],
        compiler_params=_cparams(("parallel",)),
        name="post_norm",
    )(h, y, p, g_post.reshape(1, d), w_ple_bf)


def _shift_window(c_ref, nxt_ref, new_ref, o_ref, cache_chunks):
    step = pl.program_id(0) * pl.num_programs(1) + pl.program_id(1)
    is_last = step % cache_chunks == cache_chunks - 1
    t = c_ref.shape[0]
    o_ref[0:t - 1] = c_ref[1:t]
    o_ref[t - 1] = jnp.where(is_last, new_ref[...], nxt_ref[0])


def _shift_specs(cache, new, inner, steps):
    depth, nb, win = cache.shape[:3]
    assert (depth * nb * win) % steps == 0
    rows = depth * nb * win // steps
    assert win % rows == 0
    cpw = win // rows

    def where(a, b):
        s = a * inner + b
        return s // (nb * cpw), (s // cpw) % nb, s % cpw

    tail = cache.shape[3:]
    zeros = (0,) * len(tail)
    chunk = pl.BlockSpec((None, None, rows, *tail), lambda a, b: (*where(a, b), *zeros))
    nxt = pl.BlockSpec((None, None, 1, *tail),
                       lambda a, b: (*where(a, b)[:2], jnp.minimum((where(a, b)[2] + 1) * rows, win - 1), *zeros))
    newspec = pl.BlockSpec((None, None, *tail), lambda a, b: (*where(a, b)[:2], *zeros))
    return [chunk, nxt, newspec], [cache, cache, new], chunk, cpw


def _attn_kernel(*refs, seq, cache_chunks, first_layer):
    q_ref, k_ref, v_ref, g_ref, trow_ref = refs[:5]
    ni = 5 + (3 if cache_chunks else 0) + (0 if first_layer else 2)
    o_ref, ko_ref, vo_ref = refs[ni:ni + 3]
    kb_ref, vb_ref, s_ref, tab_ref = refs[-4:]
    bid = pl.program_id(1)
    if cache_chunks:
        _shift_window(*refs[5:8], refs[ni + 3], cache_chunks)

    @pl.when(bid == 0)
    def _():
        tab_ref[...] = pltpu.roll(jnp.broadcast_to(trow_ref[...], (QBLK, TAB_P)), 0, 1, stride=1, stride_axis=0)

    k = k_ref[...]
    v = v_ref[...]
    kb_ref[...] = k.astype(BF16)
    vb_ref[...] = v.astype(BF16)
    if first_layer:
        ko_ref[0], vo_ref[0] = k, v
        ko_ref[1] = jnp.zeros(k.shape, k.dtype)
        vo_ref[1] = jnp.zeros(v.shape, v.dtype)
    else:
        ko_ref[...] = k
        vo_ref[...] = v
    nblk = seq // QBLK
    for i in range(nblk):
        q = (q_ref[i * QBLK:(i + 1) * QBLK, :] * Q_SCALE).astype(BF16)
        nkeys = (i + 1) * QBLK
        tab0 = (nblk - i) * QBLK
        chunks = [(c, min(ATT_CHUNK, nkeys - c)) for c in range(0, nkeys, ATT_CHUNK)]
        mx = jnp.full((QBLK, LANE), NEG, F32)
        for c, w in chunks:
            s = lax.dot_general(q, kb_ref[c:c + w, :], (((1,), (1,)), ((), ())), preferred_element_type=F32)
            s = s + tab_ref[:, tab0 + c:tab0 + c + w]
            s_ref[i % ATT_SBUF, :, c:c + w] = s
            for t in range(0, w, LANE):
                mx = jnp.maximum(mx, s[:, t:t + LANE])
        m = jnp.max(mx, axis=-1, keepdims=True)
        ls = jnp.zeros((QBLK, LANE), F32)
        acc = jnp.zeros((QBLK, HEAD_DIM), F32)
        for c, w in chunks:
            p = jnp.exp2(s_ref[i % ATT_SBUF, :, c:c + w] - m)
            for t in range(0, w, LANE):
                ls = ls + p[:, t:t + LANE]
            acc = acc + jnp.dot(p.astype(BF16), vb_ref[c:c + w, :], preferred_element_type=F32)
        l = jnp.sum(ls, axis=-1, keepdims=True)
        g = g_ref[i * QBLK:(i + 1) * QBLK, :]
        o = acc * (1.0 / l)
        o_ref[i * QBLK:(i + 1) * QBLK, :] = (o * (g * jax.nn.sigmoid(g))).astype(o_ref.dtype)


def _attn_prompt(qkv, u4, trow, layer, batch, seq, stacks=None, cache=None):
    m = qkv.shape[0]
    first = stacks is None
    assert first == (layer == 0) and DEPTH == 2
    blk = lambda h, b: (b, h)
    col = lambda part: pl.BlockSpec((seq, HEAD_DIM), lambda h, b: (b, part * N_HEADS + h))
    args = [qkv, qkv, qkv, u4, trow]
    in_specs = [col(0), col(1), col(2), pl.BlockSpec((seq, HEAD_DIM), blk),
                pl.BlockSpec((None, 1, TAB_P), lambda h, b: (h, 0, 0))]
    stack_shape = jax.ShapeDtypeStruct((DEPTH, m, ATT_W), F32)
    if first:
        stack_spec = pl.BlockSpec((DEPTH, seq, HEAD_DIM), lambda h, b: (0, b, h))
    else:
        stack_spec = pl.BlockSpec((None, seq, HEAD_DIM), lambda h, b: (layer, b, h))
    out_specs = [pl.BlockSpec((seq, HEAD_DIM), blk), stack_spec, stack_spec]
    out_shape = [jax.ShapeDtypeStruct((m, ATT_W), BF16), stack_shape, stack_shape]
    cpw = 0
    if cache is not None:
        specs, cargs, ospec, cpw = _shift_specs(*cache, inner=batch, steps=N_HEADS * batch)
        args += cargs
        in_specs += specs
        out_specs.append(ospec)
        out_shape.append(jax.ShapeDtypeStruct(cache[0].shape, cache[0].dtype))
    aliases = {}
    if not first:
        aliases = {len(args): 1, len(args) + 1: 2}
        args += list(stacks)
        in_specs += [pl.BlockSpec(memory_space=pl.ANY)] * 2
    return pl.pallas_call(
        functools.partial(_attn_kernel, seq=seq, cache_chunks=cpw, first_layer=first),
        input_output_aliases=aliases,
        grid=(N_HEADS, batch),
        in_specs=in_specs,
        out_specs=out_specs,
        out_shape=out_shape,
        scratch_shapes=[pltpu.VMEM((seq, HEAD_DIM), BF16), pltpu.VMEM((seq, HEAD_DIM), BF16),
                        pltpu.VMEM((ATT_SBUF, QBLK, seq), F32), pltpu.VMEM((QBLK, TAB_P), F32)],
        compiler_params=_cparams(("arbitrary", "arbitrary")),
        name="attn_prompt",
    )(*args)


CONV_TB = 256
CONV_HIST = 32
CONV_RC = 32
NCH = CONV_C // LANE


def _conv_kernel(ga_ref, gb_ref, gate_ref, w_ref, cb_ref, lg_ref, lb_ref, yb_ref, st_ref, ucat_ref, y_ref):
    t = pl.program_id(1)
    tb = CONV_TB

    @pl.when(t == 0)
    def _():
        ucat_ref[:, 0:CONV_HIST, :] = jnp.zeros((NCH, CONV_HIST, LANE), F32)

    @pl.when(t > 0)
    def _():
        ucat_ref[:, 0:CONV_HIST, :] = ucat_ref[:, tb:tb + CONV_HIST, :]

    for c in range(NCH):
        sl = slice(c * LANE, (c + 1) * LANE)
        ga = ga_ref[:, sl]
        ucat_ref[c, CONV_HIST:CONV_HIST + tb, :] = ga * _sigmoid(gb_ref[:, sl])

    off = CONV_HIST - (CONV_K - 1)

    def chunk(c, carry):
        wc = w_ref[c]
        bias = cb_ref[c]
        for r in range(0, tb, CONV_RC):
            acc = jnp.broadcast_to(bias, (CONV_RC, LANE))
            for k in range(CONV_K):
                acc = acc + wc[k:k + 1, :] * ucat_ref[c, r + off + k:r + off + k + CONV_RC, :]
            y_ref[c, r:r + CONV_RC, :] = acc
        return carry

    lax.fori_loop(0, NCH, chunk, 0)

    ssum = jnp.zeros((tb, LANE), F32)
    for c in range(NCH):
        ssum = ssum + y_ref[c]
    mu = jnp.sum(ssum, axis=-1, keepdims=True) * (1.0 / CONV_C)
    sq = jnp.zeros((tb, LANE), F32)
    for c in range(NCH):
        d = y_ref[c] - mu
        sq = sq + d * d
    rstd = lax.rsqrt(jnp.sum(sq, axis=-1, keepdims=True) * (1.0 / CONV_C) + EPS)
    for c in range(NCH):
        sl = slice(c * LANE, (c + 1) * LANE)
        z = (y_ref[c] - mu) * rstd * lg_ref[c] + lb_ref[c]
        g = gate_ref[:, sl]
        yb_ref[:, sl] = (z * _sigmoid(z) * (g * _sigmoid(g))).astype(yb_ref.dtype)

    @pl.when(t == pl.num_programs(1) - 1)
    def _():
        for c in range(NCH):
            st_ref[:, c * LANE:(c + 1) * LANE] = ucat_ref[c, CONV_HIST + tb - (CONV_K - 1):CONV_HIST + tb, :]


def _chunked(v):
    return v.reshape(v.shape[0], NCH, LANE).transpose(1, 0, 2)


def _conv_prompt(u4, conv_w, conv_b, ln_g, ln_b, batch, seq):
    m = u4.shape[0]
    nt = seq // CONV_TB
    wpad = jnp.zeros((CONV_HIST, CONV_C), F32).at[:CONV_K].set(conv_w)
    col = lambda j: pl.BlockSpec((CONV_TB, CONV_C), lambda b, t, j=j: (b * nt + t, j))
    vec = pl.BlockSpec((NCH, 1, LANE), lambda b, t: (0, 0, 0))
    return pl.pallas_call(
        _conv_kernel,
        grid=(batch, nt),
        in_specs=[col(1), col(2), col(3), pl.BlockSpec((NCH, CONV_HIST, LANE), lambda b, t: (0, 0, 0)), vec, vec, vec],
        out_specs=[pl.BlockSpec((CONV_TB, CONV_C), lambda b, t: (b * nt + t, 0)),
                   pl.BlockSpec((None, CONV_K - 1, CONV_C), lambda b, t: (b, 0, 0))],
        out_shape=[jax.ShapeDtypeStruct((m, CONV_C), BF16),
                   jax.ShapeDtypeStruct((batch, CONV_K - 1, CONV_C), F32)],
        scratch_shapes=[pltpu.VMEM((NCH, CONV_HIST + CONV_TB, LANE), F32), pltpu.VMEM((NCH, CONV_TB, LANE), F32)],
        compiler_params=_cparams(("parallel", "arbitrary")),
        name="conv_prompt",
    )(u4, u4, u4, _chunked(wpad), _chunked(conv_b.reshape(1, -1)), _chunked(ln_g.reshape(1, -1)),
      _chunked(ln_b.reshape(1, -1)))


def _conv_s_kernel(u4_ref, st_ref, w_ref, cb_ref, lg_ref, lb_ref, yb_ref, so_ref, *, nb):
    ga = u4_ref[:, CONV_C:2 * CONV_C]
    u = ga * jax.nn.sigmoid(u4_ref[:, 2 * CONV_C:3 * CONV_C])
    gate = u4_ref[:, 3 * CONV_C:4 * CONV_C]
    yb_ref[...] = jnp.zeros(yb_ref.shape, yb_ref.dtype)
    for b in range(nb):
        st = st_ref[b]
        un = u[b:b + 1, :]
        y = (jnp.sum(st * w_ref[0:CONV_K - 1, :], axis=0, keepdims=True)
             + un * w_ref[CONV_K - 1:CONV_K, :] + cb_ref[...])
        mu = jnp.mean(y, axis=-1, keepdims=True)
        d = y - mu
        z = d * lax.rsqrt(jnp.mean(d * d, axis=-1, keepdims=True) + EPS) * lg_ref[...] + lb_ref[...]
        g = gate[b:b + 1, :]
        yb_ref[b:b + 1, :] = (z * jax.nn.sigmoid(z) * (g * jax.nn.sigmoid(g))).astype(yb_ref.dtype)
        so_ref[b, 0:CONV_K - 2, :] = st[1:CONV_K - 1, :]
        so_ref[b, CONV_K - 2:CONV_K - 1, :] = un


def _conv_sample(u4s, state, conv_w, conv_b, ln_g, ln_b):
    nb = state.shape[0]
    return pl.pallas_call(
        functools.partial(_conv_s_kernel, nb=nb),
        out_shape=[jax.ShapeDtypeStruct((SAMPLE_ROWS, CONV_C), F32),
                   jax.ShapeDtypeStruct(state.shape, F32)],
        compiler_params=pltpu.CompilerParams(vmem_limit_bytes=VMEM_LIMIT),
        name="conv_sample",
    )(u4s, state, conv_w, conv_b.reshape(1, -1), ln_g.reshape(1, -1), ln_b.reshape(1, -1))


def _attn_s_kernel(q_ref, kn_ref, vn_ref, g_ref, k1_ref, k2_ref, k3_ref, v1_ref, v2_ref, v3_ref,
                   tab_ref, tnew_ref, o_ref):
    q = q_ref[...]
    s = [jnp.sum(kr[...] * q[None], axis=-1, keepdims=True) + tab_ref[p]
         for p, kr in enumerate((k1_ref, k2_ref, k3_ref))]
    s_new = jnp.sum(q * kn_ref[...], axis=-1, keepdims=True) + tnew_ref[...]
    m = s_new
    for sp in s:
        m = jnp.maximum(m, jnp.max(sp, axis=0))
    p_new = jnp.exp2(s_new - m)
    l = p_new
    acc = p_new * vn_ref[...]
    for sp, vr in zip(s, (v1_ref, v2_ref, v3_ref)):
        p = jnp.exp2(sp - m[None])
        l = l + jnp.sum(p, axis=0)
        acc = acc + jnp.sum(p * vr[...], axis=0)
    g = g_ref[...]
    o_ref[...] = acc * (1.0 / l) * (g * jax.nn.sigmoid(g))


def _attn_sample(qs, kn, vn, gs, cache_k, cache_v, tab_s, tab_new, layer):
    nb = qs.shape[0]
    rows = 128
    row = pl.BlockSpec((None, N_HEADS, HEAD_DIM), lambda b: (b, 0, 0))
    c1 = lambda c: c.reshape(DEPTH, nb, WIN_MAX // rows, rows, N_HEADS, HEAD_DIM)
    c2 = lambda c: c.reshape(DEPTH, nb, WIN_MAX // 4, 4, N_HEADS, HEAD_DIM)
    c3 = lambda c: c.reshape(DEPTH, nb, WIN_MAX // 16, 16, N_HEADS, HEAD_DIM)
    s1 = pl.BlockSpec((None, None, None, rows, N_HEADS, HEAD_DIM), lambda b: (layer, b, WIN_MAX // rows - 1, 0, 0, 0))
    s2 = pl.BlockSpec((None, None, rows, None, N_HEADS, HEAD_DIM), lambda b: (layer, b, 3, 0, 0, 0))
    s3 = pl.BlockSpec((None, None, rows, None, N_HEADS, HEAD_DIM), lambda b: (layer, b, 0, 0, 0, 0))
    return pl.pallas_call(
        _attn_s_kernel,
        grid=(nb,),
        in_specs=[row, row, row, row, s1, s2, s3, s1, s2, s3,
                  pl.BlockSpec((3, rows, N_HEADS, 1), lambda b: (0, 0, 0, 0)),
                  pl.BlockSpec((N_HEADS, 1), lambda b: (0, 0))],
        out_specs=row,
        out_shape=jax.ShapeDtypeStruct((nb, N_HEADS, HEAD_DIM), F32),
        compiler_params=_cparams(("parallel",)),
        name="attn_sample",
    )(qs, kn, vn, gs, c1(cache_k), c2(cache_k), c3(cache_k), c1(cache_v), c2(cache_v), c3(cache_v),
      tab_s, tab_new)


def _rel_bucket(dist):
    max_exact = N_BUCKETS // 2
    df = jnp.maximum(dist, 1).astype(F32)
    large = max_exact + (jnp.log(df / max_exact) / math.log(MAX_DISTANCE / max_exact)
                         * (N_BUCKETS - max_exact)).astype(jnp.int32)
    large = jnp.minimum(large, N_BUCKETS - 1)
    return jnp.where(dist < max_exact, dist, large)


def _bias_tables(rel_bias):
    dist = jnp.arange(WIN_MAX + 1, dtype=jnp.int32)
    bias = rel_bias[_rel_bucket(dist)].astype(F32).T
    mult = ((dist <= 128).astype(F32) + ((dist % 4 == 0) & (dist <= 512)).astype(F32)
            + (dist % 16 == 0).astype(F32))
    t1d = jnp.where(mult > 0, (bias + jnp.log(jnp.maximum(mult, 1.0))) * LOG2E, NEG)
    trow = jnp.full((N_HEADS, 1, TAB_P), NEG, F32).at[:, 0, :WIN_MAX + 1].set(t1d[:, ::-1])
    j = jnp.arange(128, dtype=jnp.int32)
    d_rows = jnp.stack([128 - j, 512 - 4 * j, 2048 - 16 * j])
    tab_s = (bias[:, d_rows] * LOG2E).transpose(1, 2, 0)[..., None]
    tab_new = ((bias[:, 0] + math.log(3.0)) * LOG2E).reshape(N_HEADS, 1)
    return trow, tab_s, tab_new


def kernel(x_prompt, x_sample, cache_k, cache_v, state_conv, p_prompt, p_sample, rel_bias, g_pre, w_in, conv_w,
           conv_b, ln_g, ln_b, w_out, g_post, w_ple, g_ple, w_pg, b_pg):
    batch, seq, d = x_prompt.shape
    nb = x_sample.shape[0]
    m = batch * seq
    pad = lambda a: jnp.zeros((SAMPLE_ROWS, a.size // nb), a.dtype).at[:nb].set(a.reshape(nb, -1))
    heads = lambda a: a[:nb, :ATT_W].reshape(nb, N_HEADS, HEAD_DIM)

    hp = x_prompt.reshape(m, d)
    hs = pad(x_sample)
    trow, tab_s, tab_new = _bias_tables(rel_bias)
    p_all = p_prompt.reshape(DEPTH, m, PLE_DIM)
    ps_all = jnp.stack([pad(p_sample[l]) for l in range(DEPTH)])
    wple_bf = w_ple.astype(BF16)
    stacks = new_k = new_v = None
    cp_l, cs_l, kn_l, vn_l = [], [], [], []
    for l in range(DEPTH):
        xn = _rmsnorm(hp, g_pre[l], 512)
        xns = _rmsnorm(hs, g_pre[l], SAMPLE_ROWS)
        qkv, qkvs = _mm([xn], [xns], w_in, l, 0, 3 * ATT_W, name="proj_qkv")
        qs = heads(qkvs) * Q_SCALE
        kn_l.append(heads(qkvs[:, ATT_W:]))
        vn_l.append(heads(qkvs[:, 2 * ATT_W:]))
        if l == DEPTH - 1:
            u4, u4s, new_v = _mm([xn], [xns], w_in, l, 3 * ATT_W, ATT_W + 3 * CONV_C, name="proj_rest",
                                 cache=(cache_v, jnp.stack(vn_l)))
            ya, *stacks, new_k = _attn_prompt(qkv, u4, trow, l, batch, seq, stacks=stacks,
                                              cache=(cache_k, jnp.stack(kn_l)))
        else:
            u4, u4s = _mm([xn], [xns], w_in, l, 3 * ATT_W, ATT_W + 3 * CONV_C, name="proj_rest")
            ya, *stacks = _attn_prompt(qkv, u4, trow, l, batch, seq, stacks=stacks)
        yb, cstate = _conv_prompt(u4, conv_w[l], conv_b[l], ln_g[l], ln_b[l], batch, seq)

        yas = _attn_sample(qs, kn_l[l], vn_l[l], heads(u4s), cache_k, cache_v, tab_s, tab_new, l)
        ybs, cstate_s = _conv_sample(u4s, state_conv[l], conv_w[l], conv_b[l], ln_g[l], ln_b[l])

        y, ys = _mm([ya, yb], [pad(yas).astype(BF16), ybs.astype(BF16)], w_out, l, 0, d, name="proj_out")
        h1b, rs, pn = _post(hp, y, p_all, l, g_post[l], wple_bf[l], 512)
        h1bs, rss, pns = _post(hs, ys, ps_all, l, g_post[l], wple_bf[l], SAMPLE_ROWS)
        hp, hs = _gate_mm(h1b, h1bs, w_pg, b_pg, l, g_post[l], wple_bf[l], g_ple[l],
                          (hp, y, rs, pn), (hs, ys, rss, pns))
        cp_l.append(cstate)
        cs_l.append(cstate_s)

    kv_shape = (DEPTH, batch, seq, N_HEADS, HEAD_DIM)
    return (hp.reshape(batch, seq, d), hs[:nb].reshape(nb, 1, d),
            stacks[0].reshape(kv_shape), stacks[1].reshape(kv_shape), jnp.stack(cp_l),
            new_k, new_v, jnp.stack(cs_l))
```

```python
import functools
import math

import jax
import jax.numpy as jnp
from jax import lax
from jax.experimental import pallas as pl
from jax.experimental.pallas import tpu as pltpu

D_MODEL = 4096
HEAD_DIM = 128
ATT_W = 2048
N_HEADS = 16
CONV_C = 2048
CONV_K = 31
PLE_DIM = 256
N_BUCKETS = 32
MAX_DISTANCE = 2048
WIN_MAX = 2048
EPS = 1e-6
NEG = -1e30
LOG2E = math.log2(math.e)
Q_SCALE = HEAD_DIM ** -0.5 * LOG2E
DEPTH = 2

F32 = jnp.float32
BF16 = jnp.bfloat16

VMEM_LIMIT = 62 * 1024 * 1024
SAMPLE_ROWS = 16
LANE = 128
QBLK = 256
ATT_CHUNK = 512
ATT_SBUF = 2
TAB_W = WIN_MAX + QBLK
TAB_P = TAB_W + QBLK


def _sigmoid(x):
    return 0.5 * jnp.tanh(0.5 * x) + 0.5


def _cparams(sem):
    return pltpu.CompilerParams(dimension_semantics=sem, vmem_limit_bytes=VMEM_LIMIT)


def _rmsnorm_kernel(x_ref, g_ref, o_ref):
    x = x_ref[...]
    r = x * lax.rsqrt(jnp.mean(x * x, axis=-1, keepdims=True) + EPS)
    o_ref[...] = (r * g_ref[...]).astype(o_ref.dtype)


def _rmsnorm(x, g, tm):
    m, d = x.shape
    return pl.pallas_call(
        _rmsnorm_kernel,
        grid=(m // tm,),
        in_specs=[pl.BlockSpec((tm, d), lambda i: (i, 0)), pl.BlockSpec((1, d), lambda i: (0, 0))],
        out_specs=pl.BlockSpec((tm, d), lambda i: (i, 0)),
        out_shape=jax.ShapeDtypeStruct((m, d), BF16),
        compiler_params=_cparams(("parallel",)),
        name="rmsnorm",
    )(x, g.reshape(1, d))


MM_TM = 2048
MM_TN = 256
MM_RB = 256


def _sample_spec(ms, tn):
    return pl.BlockSpec((None, ms, tn), lambda i, j: (i, 0, j))


def _mm_kernel(*refs, nparts, cache_chunks):
    xs = refs[:nparts]
    ss = refs[nparts:2 * nparts]
    ws = refs[2 * nparts:3 * nparts]
    no = 3 * nparts
    shift_refs = None
    if cache_chunks:
        shift_refs = (*refs[no:no + 3], refs[no + 5])
        no += 3
    o_ref, os_ref = refs[no:no + 2]
    wbs = refs[no + (3 if cache_chunks else 2):]
    for w_ref, wb_ref in zip(ws, wbs):
        wb_ref[...] = w_ref[...].astype(BF16)

    def product(parts, rows):
        acc = None
        for x_ref, wb_ref in zip(parts, wbs):
            d = jnp.dot(x_ref[rows, :], wb_ref[...], preferred_element_type=F32)
            acc = d if acc is None else acc + d
        return acc

    tm = xs[0].shape[0]
    for r in range(0, tm, min(tm, MM_RB)):
        rows = slice(r, r + min(tm, MM_RB))
        o_ref[rows, :] = product(xs, rows).astype(o_ref.dtype)
    if shift_refs is not None:
        _shift_window(*shift_refs, cache_chunks)

    @pl.when(pl.program_id(0) == 0)
    def _():
        os_ref[...] = product(ss, slice(None)).astype(os_ref.dtype)

    @pl.when(pl.program_id(0) > 0)
    def _():
        os_ref[...] = jnp.zeros(os_ref.shape, os_ref.dtype)


def _mm(x_parts, s_parts, w, layer, col0, ncols, *, out_dtype=F32, tm=MM_TM, tn=MM_TN, name="mm", cache=None):
    nparts = len(x_parts)
    m = x_parts[0].shape[0]
    ms = s_parts[0].shape[0]
    kparts = [x.shape[1] for x in x_parts]
    assert sum(kparts) == w.shape[1] and col0 % tn == 0 and ncols % tn == 0 and m % tm == 0
    jb0 = col0 // tn
    koff = [sum(kparts[:p]) for p in range(nparts)]
    in_specs = (
        [pl.BlockSpec((tm, kp), lambda i, j: (i, 0)) for kp in kparts]
        + [pl.BlockSpec((ms, kp), lambda i, j: (0, 0)) for kp in kparts]
        + [pl.BlockSpec((None, kp, tn), functools.partial(lambda i, j, kb: (layer, kb, jb0 + j), kb=ko // kp))
           for kp, ko in zip(kparts, koff)]
    )
    args = [*x_parts, *s_parts, *([w] * nparts)]
    out_specs = [pl.BlockSpec((tm, tn), lambda i, j: (i, j)), _sample_spec(ms, tn)]
    out_shape = [jax.ShapeDtypeStruct((m, ncols), out_dtype),
                 jax.ShapeDtypeStruct((m // tm, ms, ncols), out_dtype)]
    cpw = 0
    if cache is not None:
        specs, cargs, ospec, cpw = _shift_specs(*cache, inner=ncols // tn, steps=(m // tm) * (ncols // tn))
        args += cargs
        in_specs += specs
        out_specs.append(ospec)
        out_shape.append(jax.ShapeDtypeStruct(cache[0].shape, cache[0].dtype))
    out, out_s, *shifted = pl.pallas_call(
        functools.partial(_mm_kernel, nparts=nparts, cache_chunks=cpw),
        grid=(m // tm, ncols // tn),
        in_specs=in_specs,
        out_specs=out_specs,
        out_shape=out_shape,
        scratch_shapes=[pltpu.VMEM((kp, tn), BF16) for kp in kparts],
        compiler_params=_cparams(("arbitrary", "arbitrary")),
        name=name,
    )(*args)
    return (out, out_s[0], *shifted)


def _gate_update(x_ref, wb_ref, b_ref, gpost_ref, wple_ref, gple_ref, h_ref, y_ref, rs_ref, pn_ref, rows):
    acc = jnp.dot(x_ref[rows, :], wb_ref[...], preferred_element_type=F32)
    pen = jnp.dot(pn_ref[rows, :], wple_ref[...], preferred_element_type=F32) * gple_ref[...]
    rs = jnp.concatenate([rs_ref[rows, :]] * (acc.shape[1] // LANE), axis=1)
    h1 = h_ref[rows, :] + y_ref[rows, :] * rs * gpost_ref[...]
    return h1 + jax.nn.sigmoid(acc + b_ref[...]) * pen


def _gate_kernel(x_ref, s_ref, w_ref, b_ref, gpost_ref, wple_ref, gple_ref, h_ref, y_ref, rs_ref, pn_ref,
                 hs_ref, ys_ref, rss_ref, pns_ref, o_ref, os_ref, wb_ref):
    wb_ref[...] = w_ref[...].astype(BF16)
    shared = (wb_ref, b_ref, gpost_ref, wple_ref, gple_ref)
    tm = x_ref.shape[0]
    for r in range(0, tm, min(tm, MM_RB)):
        rows = slice(r, r + min(tm, MM_RB))
        o_ref[rows, :] = _gate_update(x_ref, *shared, h_ref, y_ref, rs_ref, pn_ref, rows)

    @pl.when(pl.program_id(0) == 0)
    def _():
        os_ref[...] = _gate_update(s_ref, *shared, hs_ref, ys_ref, rss_ref, pns_ref, slice(None))

    @pl.when(pl.program_id(0) > 0)
    def _():
        os_ref[...] = jnp.zeros(os_ref.shape, os_ref.dtype)


def _gate_mm(x, s, w, b, layer, g_post, w_ple_bf, g_ple, prompt, sample, *, tm=MM_TM, tn=MM_TN):
    m, k = x.shape
    ms = s.shape[0]
    n = w.shape[2]
    vec = pl.BlockSpec((1, tn), lambda i, j: (0, j))

    def group(rows, row_block):
        tile = pl.BlockSpec((rows, tn), lambda i, j: (row_block(i), j))
        return [tile, tile, pl.BlockSpec((rows, LANE), lambda i, j: (row_block(i), 0)),
                pl.BlockSpec((rows, PLE_DIM), lambda i, j: (row_block(i), 0))]

    out, out_s = pl.pallas_call(
        _gate_kernel,
        grid=(m // tm, n // tn),
        in_specs=[pl.BlockSpec((tm, k), lambda i, j: (i, 0)), pl.BlockSpec((ms, k), lambda i, j: (0, 0)),
                  pl.BlockSpec((None, k, tn), lambda i, j: (layer, 0, j)),
                  pl.BlockSpec((None, 1, tn), lambda i, j: (layer, 0, j)),
                  vec, pl.BlockSpec((PLE_DIM, tn), lambda i, j: (0, j)), vec,
                  *group(tm, lambda i: i), *group(ms, lambda i: 0)],
        out_specs=[pl.BlockSpec((tm, tn), lambda i, j: (i, j)), _sample_spec(ms, tn)],
        out_shape=[jax.ShapeDtypeStruct((m, n), F32), jax.ShapeDtypeStruct((m // tm, ms, n), F32)],
        scratch_shapes=[pltpu.VMEM((k, tn), BF16)],
        compiler_params=_cparams(("arbitrary", "arbitrary")),
        name="gate_mm",
    )(x, s, w, b.reshape(DEPTH, 1, n), g_post.reshape(1, n), w_ple_bf, g_ple.reshape(1, n), *prompt, *sample)
    return out, out_s[0]


def _post_kernel(h_ref, y_ref, p_ref, gpost_ref, wple_ref, h1b_ref, rs_ref, pn_ref):
    y = y_ref[...]
    rs_y = lax.rsqrt(jnp.mean(y * y, axis=-1, keepdims=True) + EPS)
    h1b_ref[...] = (h_ref[...] + y * rs_y * gpost_ref[...]).astype(BF16)
    rs_ref[...] = jnp.broadcast_to(rs_y, rs_ref.shape)
    p = p_ref[...]
    pe = jnp.dot(p.astype(BF16), wple_ref[...], preferred_element_type=F32)
    rs = lax.rsqrt(jnp.mean(pe * pe, axis=-1, keepdims=True) + EPS)
    pn_ref[...] = (p * rs).astype(BF16)


def _post(h, y, p, layer, g_post, w_ple_bf, tm):
    m, d = h.shape
    row = pl.BlockSpec((tm, d), lambda i: (i, 0))
    vec = pl.BlockSpec((1, d), lambda i: (0, 0))
    return pl.pallas_call(
        _post_kernel,
        grid=(m // tm,),
        in_specs=[row, row, pl.BlockSpec((None, tm, PLE_DIM), lambda i: (layer, i, 0)), vec,
                  pl.BlockSpec((PLE_DIM, d), lambda i: (0, 0))],
        out_specs=[row, pl.BlockSpec((tm, LANE), lambda i: (i, 0)), pl.BlockSpec((tm, PLE_DIM), lambda i: (i, 0))],
        out_shape=[jax.ShapeDtypeStruct((m, d), BF16), jax.ShapeDtypeStruct((m, LANE), F32),
                   jax.ShapeDtypeStruct((m, PLE_DIM), BF16)],
        compiler_params=_cparams(("parallel",)),
        name="post_norm",
    )(h, y, p, g_post.reshape(1, d), w_ple_bf)


def _shift_window(c_ref, nxt_ref, new_ref, o_ref, cache_chunks):
    step = pl.program_id(0) * pl.num_programs(1) + pl.program_id(1)
    is_last = step % cache_chunks == cache_chunks - 1
    t = c_ref.shape[0]
    o_ref[0:t - 1] = c_ref[1:t]
    o_ref[t - 1] = jnp.where(is_last, new_ref[...], nxt_ref[0])


def _shift_specs(cache, new, inner, steps):
    depth, nb, win = cache.shape[:3]
    assert (depth * nb * win) % steps == 0
    rows = depth * nb * win // steps
    assert win % rows == 0
    cpw = win // rows

    def where(a, b):
        s = a * inner + b
        return s // (nb * cpw), (s // cpw) % nb, s % cpw

    tail = cache.shape[3:]
    zeros = (0,) * len(tail)
    chunk = pl.BlockSpec((None, None, rows, *tail), lambda a, b: (*where(a, b), *zeros))
    nxt = pl.BlockSpec((None, None, 1, *tail),
                       lambda a, b: (*where(a, b)[:2], jnp.minimum((where(a, b)[2] + 1) * rows, win - 1), *zeros))
    newspec = pl.BlockSpec((None, None, *tail), lambda a, b: (*where(a, b)[:2], *zeros))
    return [chunk, nxt, newspec], [cache, cache, new], chunk, cpw


def _attn_kernel(*refs, seq, cache_chunks, first_layer):
    q_ref, k_ref, v_ref, g_ref, trow_ref = refs[:5]
    ni = 5 + (3 if cache_chunks else 0) + (0 if first_layer else 2)
    o_ref, ko_ref, vo_ref = refs[ni:ni + 3]
    kb_ref, vb_ref, s_ref, tab_ref = refs[-4:]
    bid = pl.program_id(1)
    if cache_chunks:
        _shift_window(*refs[5:8], refs[ni + 3], cache_chunks)

    @pl.when(bid == 0)
    def _():
        tab_ref[...] = pltpu.roll(jnp.broadcast_to(trow_ref[...], (QBLK, TAB_P)), 0, 1, stride=1, stride_axis=0)

    k = k_ref[...]
    v = v_ref[...]
    kb_ref[...] = k.astype(BF16)
    vb_ref[:, 0:HEAD_DIM] = v.astype(BF16)
    vb_ref[:, HEAD_DIM:2 * HEAD_DIM] = jnp.ones(v.shape, BF16)
    if first_layer:
        ko_ref[0], vo_ref[0] = k, v
        ko_ref[1] = jnp.zeros(k.shape, k.dtype)
        vo_ref[1] = jnp.zeros(v.shape, v.dtype)
    else:
        ko_ref[...] = k
        vo_ref[...] = v
    nblk = seq // QBLK
    for i in range(nblk):
        q = (q_ref[i * QBLK:(i + 1) * QBLK, :] * Q_SCALE).astype(BF16)
        nkeys = (i + 1) * QBLK
        tab0 = (nblk - i) * QBLK
        chunks = [(c, min(ATT_CHUNK, nkeys - c)) for c in range(0, nkeys, ATT_CHUNK)]
        mx = jnp.full((QBLK, LANE), NEG, F32)
        for c, w in chunks:
            s = lax.dot_general(q, kb_ref[c:c + w, :], (((1,), (1,)), ((), ())), preferred_element_type=F32)
            s = s + tab_ref[:, tab0 + c:tab0 + c + w]
            s_ref[i % ATT_SBUF, :, c:c + w] = s
            for t in range(0, w, LANE):
                mx = jnp.maximum(mx, s[:, t:t + LANE])
        m = jnp.max(mx, axis=-1, keepdims=True)
        acc = jnp.zeros((QBLK, 2 * HEAD_DIM), F32)
        for c, w in chunks:
            p = jnp.exp2(s_ref[i % ATT_SBUF, :, c:c + w] - m)
            acc = acc + jnp.dot(p.astype(BF16), vb_ref[c:c + w, :], preferred_element_type=F32)
        g = g_ref[i * QBLK:(i + 1) * QBLK, :]
        o = acc[:, 0:HEAD_DIM] * (1.0 / acc[:, HEAD_DIM:2 * HEAD_DIM])
        o_ref[i * QBLK:(i + 1) * QBLK, :] = (o * (g * jax.nn.sigmoid(g))).astype(o_ref.dtype)


def _attn_prompt(qkv, u4, trow, layer, batch, seq, stacks=None, cache=None):
    m = qkv.shape[0]
    first = stacks is None
    assert first == (layer == 0) and DEPTH == 2
    blk = lambda h, b: (b, h)
    col = lambda part: pl.BlockSpec((seq, HEAD_DIM), lambda h, b: (b, part * N_HEADS + h))
    args = [qkv, qkv, qkv, u4, trow]
    in_specs = [col(0), col(1), col(2), pl.BlockSpec((seq, HEAD_DIM), blk),
                pl.BlockSpec((None, 1, TAB_P), lambda h, b: (h, 0, 0))]
    stack_shape = jax.ShapeDtypeStruct((DEPTH, m, ATT_W), F32)
    if first:
        stack_spec = pl.BlockSpec((DEPTH, seq, HEAD_DIM), lambda h, b: (0, b, h))
    else:
        stack_spec = pl.BlockSpec((None, seq, HEAD_DIM), lambda h, b: (layer, b, h))
    out_specs = [pl.BlockSpec((seq, HEAD_DIM), blk), stack_spec, stack_spec]
    out_shape = [jax.ShapeDtypeStruct((m, ATT_W), BF16), stack_shape, stack_shape]
    cpw = 0
    if cache is not None:
        specs, cargs, ospec, cpw = _shift_specs(*cache, inner=batch, steps=N_HEADS * batch)
        args += cargs
        in_specs += specs
        out_specs.append(ospec)
        out_shape.append(jax.ShapeDtypeStruct(cache[0].shape, cache[0].dtype))
    aliases = {}
    if not first:
        aliases = {len(args): 1, len(args) + 1: 2}
        args += list(stacks)
        in_specs += [pl.BlockSpec(memory_space=pl.ANY)] * 2
    return pl.pallas_call(
        functools.partial(_attn_kernel, seq=seq, cache_chunks=cpw, first_layer=first),
        input_output_aliases=aliases,
        grid=(N_HEADS, batch),
        in_specs=in_specs,
        out_specs=out_specs,
        out_shape=out_shape,
        scratch_shapes=[pltpu.VMEM((seq, HEAD_DIM), BF16), pltpu.VMEM((seq, 2 * HEAD_DIM), BF16),
                        pltpu.VMEM((ATT_SBUF, QBLK, seq), F32), pltpu.VMEM((QBLK, TAB_P), F32)],
        compiler_params=_cparams(("arbitrary", "arbitrary")),
        name="attn_prompt",
    )(*args)


CONV_TB = 256
CONV_HIST = 32
CONV_RC = 32
NCH = CONV_C // LANE


def _conv_kernel(ga_ref, gb_ref, gate_ref, w_ref, cb_ref, lg_ref, lb_ref, yb_ref, st_ref, ucat_ref, y_ref):
    t = pl.program_id(1)
    tb = CONV_TB

    @pl.when(t == 0)
    def _():
        ucat_ref[:, 0:CONV_HIST, :] = jnp.zeros((NCH, CONV_HIST, LANE), F32)

    @pl.when(t > 0)
    def _():
        ucat_ref[:, 0:CONV_HIST, :] = ucat_ref[:, tb:tb + CONV_HIST, :]

    for c in range(NCH):
        sl = slice(c * LANE, (c + 1) * LANE)
        ga = ga_ref[:, sl]
        ucat_ref[c, CONV_HIST:CONV_HIST + tb, :] = ga * _sigmoid(gb_ref[:, sl])

    off = CONV_HIST - (CONV_K - 1)

    def chunk(c, carry):
        wc = w_ref[c]
        bias = cb_ref[c]
        for r in range(0, tb, CONV_RC):
            acc = jnp.broadcast_to(bias, (CONV_RC, LANE))
            for k in range(CONV_K):
                acc = acc + wc[k:k + 1, :] * ucat_ref[c, r + off + k:r + off + k + CONV_RC, :]
            y_ref[c, r:r + CONV_RC, :] = acc
        return carry

    lax.fori_loop(0, NCH, chunk, 0)

    ssum = jnp.zeros((tb, LANE), F32)
    for c in range(NCH):
        ssum = ssum + y_ref[c]
    mu = jnp.sum(ssum, axis=-1, keepdims=True) * (1.0 / CONV_C)
    sq = jnp.zeros((tb, LANE), F32)
    for c in range(NCH):
        d = y_ref[c] - mu
        sq = sq + d * d
    rstd = lax.rsqrt(jnp.sum(sq, axis=-1, keepdims=True) * (1.0 / CONV_C) + EPS)
    for c in range(NCH):
        sl = slice(c * LANE, (c + 1) * LANE)
        z = (y_ref[c] - mu) * rstd * lg_ref[c] + lb_ref[c]
        g = gate_ref[:, sl]
        yb_ref[:, sl] = (z * _sigmoid(z) * (g * _sigmoid(g))).astype(yb_ref.dtype)

    @pl.when(t == pl.num_programs(1) - 1)
    def _():
        for c in range(NCH):
            st_ref[:, c * LANE:(c + 1) * LANE] = ucat_ref[c, CONV_HIST + tb - (CONV_K - 1):CONV_HIST + tb, :]


def _chunked(v):
    return v.reshape(v.shape[0], NCH, LANE).transpose(1, 0, 2)


def _conv_prompt(u4, conv_w, conv_b, ln_g, ln_b, batch, seq):
    m = u4.shape[0]
    nt = seq // CONV_TB
    wpad = jnp.zeros((CONV_HIST, CONV_C), F32).at[:CONV_K].set(conv_w)
    col = lambda j: pl.BlockSpec((CONV_TB, CONV_C), lambda b, t, j=j: (b * nt + t, j))
    vec = pl.BlockSpec((NCH, 1, LANE), lambda b, t: (0, 0, 0))
    return pl.pallas_call(
        _conv_kernel,
        grid=(batch, nt),
        in_specs=[col(1), col(2), col(3), pl.BlockSpec((NCH, CONV_HIST, LANE), lambda b, t: (0, 0, 0)), vec, vec, vec],
        out_specs=[pl.BlockSpec((CONV_TB, CONV_C), lambda b, t: (b * nt + t, 0)),
                   pl.BlockSpec((None, CONV_K - 1, CONV_C), lambda b, t: (b, 0, 0))],
        out_shape=[jax.ShapeDtypeStruct((m, CONV_C), BF16),
                   jax.ShapeDtypeStruct((batch, CONV_K - 1, CONV_C), F32)],
        scratch_shapes=[pltpu.VMEM((NCH, CONV_HIST + CONV_TB, LANE), F32), pltpu.VMEM((NCH, CONV_TB, LANE), F32)],
        compiler_params=_cparams(("parallel", "arbitrary")),
        name="conv_prompt",
    )(u4, u4, u4, _chunked(wpad), _chunked(conv_b.reshape(1, -1)), _chunked(ln_g.reshape(1, -1)),
      _chunked(ln_b.reshape(1, -1)))


def _conv_s_kernel(u4_ref, st_ref, w_ref, cb_ref, lg_ref, lb_ref, yb_ref, so_ref, *, nb):
    ga = u4_ref[:, CONV_C:2 * CONV_C]
    u = ga * jax.nn.sigmoid(u4_ref[:, 2 * CONV_C:3 * CONV_C])
    gate = u4_ref[:, 3 * CONV_C:4 * CONV_C]
    yb_ref[...] = jnp.zeros(yb_ref.shape, yb_ref.dtype)
    for b in range(nb):
        st = st_ref[b]
        un = u[b:b + 1, :]
        y = (jnp.sum(st * w_ref[0:CONV_K - 1, :], axis=0, keepdims=True)
             + un * w_ref[CONV_K - 1:CONV_K, :] + cb_ref[...])
        mu = jnp.mean(y, axis=-1, keepdims=True)
        d = y - mu
        z = d * lax.rsqrt(jnp.mean(d * d, axis=-1, keepdims=True) + EPS) * lg_ref[...] + lb_ref[...]
        g = gate[b:b + 1, :]
        yb_ref[b:b + 1, :] = (z * jax.nn.sigmoid(z) * (g * jax.nn.sigmoid(g))).astype(yb_ref.dtype)
        so_ref[b, 0:CONV_K - 2, :] = st[1:CONV_K - 1, :]
        so_ref[b, CONV_K - 2:CONV_K - 1, :] = un


def _conv_sample(u4s, state, conv_w, conv_b, ln_g, ln_b):
    nb = state.shape[0]
    return pl.pallas_call(
        functools.partial(_conv_s_kernel, nb=nb),
        out_shape=[jax.ShapeDtypeStruct((SAMPLE_ROWS, CONV_C), F32),
                   jax.ShapeDtypeStruct(state.shape, F32)],
        compiler_params=pltpu.CompilerParams(vmem_limit_bytes=VMEM_LIMIT),
        name="conv_sample",
    )(u4s, state, conv_w, conv_b.reshape(1, -1), ln_g.reshape(1, -1), ln_b.reshape(1, -1))


def _attn_s_kernel(q_ref, kn_ref, vn_ref, g_ref, k1_ref, k2_ref, k3_ref, v1_ref, v2_ref, v3_ref,
                   tab_ref, tnew_ref, o_ref):
    q = q_ref[...]
    s = [jnp.sum(kr[...] * q[None], axis=-1, keepdims=True) + tab_ref[p]
         for p, kr in enumerate((k1_ref, k2_ref, k3_ref))]
    s_new = jnp.sum(q * kn_ref[...], axis=-1, keepdims=True) + tnew_ref[...]
    m = s_new
    for sp in s:
        m = jnp.maximum(m, jnp.max(sp, axis=0))
    p_new = jnp.exp2(s_new - m)
    l = p_new
    acc = p_new * vn_ref[...]
    for sp, vr in zip(s, (v1_ref, v2_ref, v3_ref)):
        p = jnp.exp2(sp - m[None])
        l = l + jnp.sum(p, axis=0)
        acc = acc + jnp.sum(p * vr[...], axis=0)
    g = g_ref[...]
    o_ref[...] = acc * (1.0 / l) * (g * jax.nn.sigmoid(g))


def _attn_sample(qs, kn, vn, gs, cache_k, cache_v, tab_s, tab_new, layer):
    nb = qs.shape[0]
    rows = 128
    row = pl.BlockSpec((None, N_HEADS, HEAD_DIM), lambda b: (b, 0, 0))
    c1 = lambda c: c.reshape(DEPTH, nb, WIN_MAX // rows, rows, N_HEADS, HEAD_DIM)
    c2 = lambda c: c.reshape(DEPTH, nb, WIN_MAX // 4, 4, N_HEADS, HEAD_DIM)
    c3 = lambda c: c.reshape(DEPTH, nb, WIN_MAX // 16, 16, N_HEADS, HEAD_DIM)
    s1 = pl.BlockSpec((None, None, None, rows, N_HEADS, HEAD_DIM), lambda b: (layer, b, WIN_MAX // rows - 1, 0, 0, 0))
    s2 = pl.BlockSpec((None, None, rows, None, N_HEADS, HEAD_DIM), lambda b: (layer, b, 3, 0, 0, 0))
    s3 = pl.BlockSpec((None, None, rows, None, N_HEADS, HEAD_DIM), lambda b: (layer, b, 0, 0, 0, 0))
    return pl.pallas_call(
        _attn_s_kernel,
        grid=(nb,),
        in_specs=[row, row, row, row, s1, s2, s3, s1, s2, s3,
                  pl.BlockSpec((3, rows, N_HEADS, 1), lambda b: (0, 0, 0, 0)),
                  pl.BlockSpec((N_HEADS, 1), lambda b: (0, 0))],
        out_specs=row,
        out_shape=jax.ShapeDtypeStruct((nb, N_HEADS, HEAD_DIM), F32),
        compiler_params=_cparams(("parallel",)),
        name="attn_sample",
    )(qs, kn, vn, gs, c1(cache_k), c2(cache_k), c3(cache_k), c1(cache_v), c2(cache_v), c3(cache_v),
      tab_s, tab_new)


def _rel_bucket(dist):
    max_exact = N_BUCKETS // 2
    df = jnp.maximum(dist, 1).astype(F32)
    large = max_exact + (jnp.log(df / max_exact) / math.log(MAX_DISTANCE / max_exact)
                         * (N_BUCKETS - max_exact)).astype(jnp.int32)
    large = jnp.minimum(large, N_BUCKETS - 1)
    return jnp.where(dist < max_exact, dist, large)


def _bias_tables(rel_bias):
    dist = jnp.arange(WIN_MAX + 1, dtype=jnp.int32)
    bias = rel_bias[_rel_bucket(dist)].astype(F32).T
    mult = ((dist <= 128).astype(F32) + ((dist % 4 == 0) & (dist <= 512)).astype(F32)
            + (dist % 16 == 0).astype(F32))
    t1d = jnp.where(mult > 0, (bias + jnp.log(jnp.maximum(mult, 1.0))) * LOG2E, NEG)
    trow = jnp.full((N_HEADS, 1, TAB_P), NEG, F32).at[:, 0, :WIN_MAX + 1].set(t1d[:, ::-1])
    j = jnp.arange(128, dtype=jnp.int32)
    d_rows = jnp.stack([128 - j, 512 - 4 * j, 2048 - 16 * j])
    tab_s = (bias[:, d_rows] * LOG2E).transpose(1, 2, 0)[..., None]
    tab_new = ((bias[:, 0] + math.log(3.0)) * LOG2E).reshape(N_HEADS, 1)
    return trow, tab_s, tab_new


def kernel(x_prompt, x_sample, cache_k, cache_v, state_conv, p_prompt, p_sample, rel_bias, g_pre, w_in, conv_w,
           conv_b, ln_g, ln_b, w_out, g_post, w_ple, g_ple, w_pg, b_pg):
    batch, seq, d = x_prompt.shape
    nb = x_sample.shape[0]
    m = batch * seq
    pad = lambda a: jnp.zeros((SAMPLE_ROWS, a.size // nb), a.dtype).at[:nb].set(a.reshape(nb, -1))
    heads = lambda a: a[:nb, :ATT_W].reshape(nb, N_HEADS, HEAD_DIM)

    hp = x_prompt.reshape(m, d)
    hs = pad(x_sample)
    trow, tab_s, tab_new = _bias_tables(rel_bias)
    p_all = p_prompt.reshape(DEPTH, m, PLE_DIM)
    ps_all = jnp.stack([pad(p_sample[l]) for l in range(DEPTH)])
    wple_bf = w_ple.astype(BF16)
    stacks = new_k = new_v = None
    cp_l, cs_l, kn_l, vn_l = [], [], [], []
    for l in range(DEPTH):
        xn = _rmsnorm(hp, g_pre[l], 512)
        xns = _rmsnorm(hs, g_pre[l], SAMPLE_ROWS)
        qkv, qkvs = _mm([xn], [xns], w_in, l, 0, 3 * ATT_W, name="proj_qkv")
        qs = heads(qkvs) * Q_SCALE
        kn_l.append(heads(qkvs[:, ATT_W:]))
        vn_l.append(heads(qkvs[:, 2 * ATT_W:]))
        if l == DEPTH - 1:
            u4, u4s, new_v = _mm([xn], [xns], w_in, l, 3 * ATT_W, ATT_W + 3 * CONV_C, name="proj_rest",
                                 cache=(cache_v, jnp.stack(vn_l)))
            ya, *stacks, new_k = _attn_prompt(qkv, u4, trow, l, batch, seq, stacks=stacks,
                                              cache=(cache_k, jnp.stack(kn_l)))
        else:
            u4, u4s = _mm([xn], [xns], w_in, l, 3 * ATT_W, ATT_W + 3 * CONV_C, name="proj_rest")
            ya, *stacks = _attn_prompt(qkv, u4, trow, l, batch, seq, stacks=stacks)
        yb, cstate = _conv_prompt(u4, conv_w[l], conv_b[l], ln_g[l], ln_b[l], batch, seq)

        yas = _attn_sample(qs, kn_l[l], vn_l[l], heads(u4s), cache_k, cache_v, tab_s, tab_new, l)
        ybs, cstate_s = _conv_sample(u4s, state_conv[l], conv_w[l], conv_b[l], ln_g[l], ln_b[l])

        y, ys = _mm([ya, yb], [pad(yas).astype(BF16), ybs.astype(BF16)], w_out, l, 0, d, name="proj_out")
        h1b, rs, pn = _post(hp, y, p_all, l, g_post[l], wple_bf[l], 512)
        h1bs, rss, pns = _post(hs, ys, ps_all, l, g_post[l], wple_bf[l], SAMPLE_ROWS)
        hp, hs = _gate_mm(h1b, h1bs, w_pg, b_pg, l, g_post[l], wple_bf[l], g_ple[l],
                          (hp, y, rs, pn), (hs, ys, rss, pns))
        cp_l.append(cstate)
        cs_l.append(cstate_s)

    kv_shape = (DEPTH, batch, seq, N_HEADS, HEAD_DIM)
    return (hp.reshape(batch, seq, d), hs[:nb].reshape(nb, 1, d),
            stacks[0].reshape(kv_shape), stacks[1].reshape(kv_shape), jnp.stack(cp_l),
            new_k, new_v, jnp.stack(cs_l))
```

```python
import functools
import math

import jax
import jax.numpy as jnp
from jax import lax
from jax.experimental import pallas as pl
from jax.experimental.pallas import tpu as pltpu

D_MODEL = 4096
HEAD_DIM = 128
ATT_W = 2048
N_HEADS = 16
CONV_C = 2048
CONV_K = 31
PLE_DIM = 256
N_BUCKETS = 32
MAX_DISTANCE = 2048
DILATED_PAIRS = ((128, 1), (512, 4), (2048, 16))
WIN_MAX = max(w for w, _ in DILATED_PAIRS)
PATTERN_KEYS = DILATED_PAIRS[0][0] // DILATED_PAIRS[0][1]
assert all(w // d == PATTERN_KEYS for w, d in DILATED_PAIRS)
EPS = 1e-6
NEG = -1e30
LOG2E = math.log2(math.e)
Q_SCALE = HEAD_DIM ** -0.5 * LOG2E
DEPTH = 2

F32 = jnp.float32
BF16 = jnp.bfloat16

VMEM_LIMIT = 62 * 1024 * 1024
SAMPLE_ROWS = 16
LANE = 128
QBLK = 256
ATT_CHUNK = 512
ATT_SBUF = 2
TAB_W = WIN_MAX + QBLK
TAB_P = TAB_W + QBLK


def _sigmoid(x):
    return 0.5 * jnp.tanh(0.5 * x) + 0.5


def _cparams(sem):
    return pltpu.CompilerParams(dimension_semantics=sem, vmem_limit_bytes=VMEM_LIMIT)


def _rmsnorm_kernel(x_ref, g_ref, o_ref):
    x = x_ref[...]
    r = x * lax.rsqrt(jnp.mean(x * x, axis=-1, keepdims=True) + EPS)
    o_ref[...] = (r * g_ref[...]).astype(o_ref.dtype)


def _rmsnorm(x, g, tm):
    m, d = x.shape
    return pl.pallas_call(
        _rmsnorm_kernel,
        grid=(m // tm,),
        in_specs=[pl.BlockSpec((tm, d), lambda i: (i, 0)), pl.BlockSpec((1, d), lambda i: (0, 0))],
        out_specs=pl.BlockSpec((tm, d), lambda i: (i, 0)),
        out_shape=jax.ShapeDtypeStruct((m, d), BF16),
        compiler_params=_cparams(("parallel",)),
        name="rmsnorm",
    )(x, g.reshape(1, d))


MM_TM = 2048
MM_TN = 256
MM_RB = 256


def _sample_spec(ms, tn):
    return pl.BlockSpec((None, ms, tn), lambda i, j: (i, 0, j))


def _mm_kernel(*refs, nparts, cache_chunks):
    xs = refs[:nparts]
    ss = refs[nparts:2 * nparts]
    ws = refs[2 * nparts:3 * nparts]
    no = 3 * nparts
    shift_refs = None
    if cache_chunks:
        shift_refs = (*refs[no:no + 3], refs[no + 5])
        no += 3
    o_ref, os_ref = refs[no:no + 2]
    wbs = refs[no + (3 if cache_chunks else 2):]
    for w_ref, wb_ref in zip(ws, wbs):
        wb_ref[...] = w_ref[...].astype(BF16)

    def product(parts, rows):
        acc = None
        for x_ref, wb_ref in zip(parts, wbs):
            d = jnp.dot(x_ref[rows, :], wb_ref[...], preferred_element_type=F32)
            acc = d if acc is None else acc + d
        return acc

    tm = xs[0].shape[0]
    for r in range(0, tm, min(tm, MM_RB)):
        rows = slice(r, r + min(tm, MM_RB))
        o_ref[rows, :] = product(xs, rows).astype(o_ref.dtype)
    if shift_refs is not None:
        _shift_window(*shift_refs, cache_chunks)

    @pl.when(pl.program_id(0) == 0)
    def _():
        os_ref[...] = product(ss, slice(None)).astype(os_ref.dtype)

    @pl.when(pl.program_id(0) > 0)
    def _():
        os_ref[...] = jnp.zeros(os_ref.shape, os_ref.dtype)


def _mm(x_parts, s_parts, w, layer, col0, ncols, *, out_dtype=F32, tm=MM_TM, tn=MM_TN, name="mm", cache=None):
    nparts = len(x_parts)
    m = x_parts[0].shape[0]
    ms = s_parts[0].shape[0]
    kparts = [x.shape[1] for x in x_parts]
    assert sum(kparts) == w.shape[1] and col0 % tn == 0 and ncols % tn == 0 and m % tm == 0
    jb0 = col0 // tn
    koff = [sum(kparts[:p]) for p in range(nparts)]
    in_specs = (
        [pl.BlockSpec((tm, kp), lambda i, j: (i, 0)) for kp in kparts]
        + [pl.BlockSpec((ms, kp), lambda i, j: (0, 0)) for kp in kparts]
        + [pl.BlockSpec((None, kp, tn), functools.partial(lambda i, j, kb: (layer, kb, jb0 + j), kb=ko // kp))
           for kp, ko in zip(kparts, koff)]
    )
    args = [*x_parts, *s_parts, *([w] * nparts)]
    out_specs = [pl.BlockSpec((tm, tn), lambda i, j: (i, j)), _sample_spec(ms, tn)]
    out_shape = [jax.ShapeDtypeStruct((m, ncols), out_dtype),
                 jax.ShapeDtypeStruct((m // tm, ms, ncols), out_dtype)]
    cpw = 0
    if cache is not None:
        specs, cargs, ospec, cpw = _shift_specs(*cache, inner=ncols // tn, steps=(m // tm) * (ncols // tn))
        args += cargs
        in_specs += specs
        out_specs.append(ospec)
        out_shape.append(jax.ShapeDtypeStruct(cache[0].shape, cache[0].dtype))
    out, out_s, *shifted = pl.pallas_call(
        functools.partial(_mm_kernel, nparts=nparts, cache_chunks=cpw),
        grid=(m // tm, ncols // tn),
        in_specs=in_specs,
        out_specs=out_specs,
        out_shape=out_shape,
        scratch_shapes=[pltpu.VMEM((kp, tn), BF16) for kp in kparts],
        compiler_params=_cparams(("arbitrary", "arbitrary")),
        name=name,
    )(*args)
    return (out, out_s[0], *shifted)


def _gate_update(x_ref, wb_ref, b_ref, gpost_ref, wple_ref, gple_ref, h_ref, y_ref, rs_ref, pn_ref, rows):
    acc = jnp.dot(x_ref[rows, :], wb_ref[...], preferred_element_type=F32)
    pen = jnp.dot(pn_ref[rows, :], wple_ref[...], preferred_element_type=F32) * gple_ref[...]
    rs = jnp.concatenate([rs_ref[rows, :]] * (acc.shape[1] // LANE), axis=1)
    h1 = h_ref[rows, :] + y_ref[rows, :] * rs * gpost_ref[...]
    return h1 + jax.nn.sigmoid(acc + b_ref[...]) * pen


def _gate_kernel(x_ref, s_ref, w_ref, b_ref, gpost_ref, wple_ref, gple_ref, h_ref, y_ref, rs_ref, pn_ref,
                 hs_ref, ys_ref, rss_ref, pns_ref, o_ref, os_ref, wb_ref):
    wb_ref[...] = w_ref[...].astype(BF16)
    shared = (wb_ref, b_ref, gpost_ref, wple_ref, gple_ref)
    tm = x_ref.shape[0]
    for r in range(0, tm, min(tm, MM_RB)):
        rows = slice(r, r + min(tm, MM_RB))
        o_ref[rows, :] = _gate_update(x_ref, *shared, h_ref, y_ref, rs_ref, pn_ref, rows)

    @pl.when(pl.program_id(0) == 0)
    def _():
        os_ref[...] = _gate_update(s_ref, *shared, hs_ref, ys_ref, rss_ref, pns_ref, slice(None))

    @pl.when(pl.program_id(0) > 0)
    def _():
        os_ref[...] = jnp.zeros(os_ref.shape, os_ref.dtype)


def _gate_mm(x, s, w, b, layer, g_post, w_ple_bf, g_ple, prompt, sample, *, tm=MM_TM, tn=MM_TN):
    m, k = x.shape
    ms = s.shape[0]
    n = w.shape[2]
    vec = pl.BlockSpec((1, tn), lambda i, j: (0, j))

    def group(rows, row_block):
        tile = pl.BlockSpec((rows, tn), lambda i, j: (row_block(i), j))
        return [tile, tile, pl.BlockSpec((rows, LANE), lambda i, j: (row_block(i), 0)),
                pl.BlockSpec((rows, PLE_DIM), lambda i, j: (row_block(i), 0))]

    out, out_s = pl.pallas_call(
        _gate_kernel,
        grid=(m // tm, n // tn),
        in_specs=[pl.BlockSpec((tm, k), lambda i, j: (i, 0)), pl.BlockSpec((ms, k), lambda i, j: (0, 0)),
                  pl.BlockSpec((None, k, tn), lambda i, j: (layer, 0, j)),
                  pl.BlockSpec((None, 1, tn), lambda i, j: (layer, 0, j)),
                  vec, pl.BlockSpec((PLE_DIM, tn), lambda i, j: (0, j)), vec,
                  *group(tm, lambda i: i), *group(ms, lambda i: 0)],
        out_specs=[pl.BlockSpec((tm, tn), lambda i, j: (i, j)), _sample_spec(ms, tn)],
        out_shape=[jax.ShapeDtypeStruct((m, n), F32), jax.ShapeDtypeStruct((m // tm, ms, n), F32)],
        scratch_shapes=[pltpu.VMEM((k, tn), BF16)],
        compiler_params=_cparams(("arbitrary", "arbitrary")),
        name="gate_mm",
    )(x, s, w, b.reshape(DEPTH, 1, n), g_post.reshape(1, n), w_ple_bf, g_ple.reshape(1, n), *prompt, *sample)
    return out, out_s[0]


def _post_kernel(h_ref, y_ref, p_ref, gpost_ref, wple_ref, h1b_ref, rs_ref, pn_ref):
    y = y_ref[...]
    rs_y = lax.rsqrt(jnp.mean(y * y, axis=-1, keepdims=True) + EPS)
    h1b_ref[...] = (h_ref[...] + y * rs_y * gpost_ref[...]).astype(BF16)
    rs_ref[...] = jnp.broadcast_to(rs_y, rs_ref.shape)
    p = p_ref[...]
    pe = jnp.dot(p.astype(BF16), wple_ref[...], preferred_element_type=F32)
    rs = lax.rsqrt(jnp.mean(pe * pe, axis=-1, keepdims=True) + EPS)
    pn_ref[...] = (p * rs).astype(BF16)


def _post(h, y, p, layer, g_post, w_ple_bf, tm):
    m, d = h.shape
    row = pl.BlockSpec((tm, d), lambda i: (i, 0))
    vec = pl.BlockSpec((1, d), lambda i: (0, 0))
    return pl.pallas_call(
        _post_kernel,
        grid=(m // tm,),
        in_specs=[row, row, pl.BlockSpec((None, tm, PLE_DIM), lambda i: (layer, i, 0)), vec,
                  pl.BlockSpec((PLE_DIM, d), lambda i: (0, 0))],
        out_specs=[row, pl.BlockSpec((tm, LANE), lambda i: (i, 0)), pl.BlockSpec((tm, PLE_DIM), lambda i: (i, 0))],
        out_shape=[jax.ShapeDtypeStruct((m, d), BF16), jax.ShapeDtypeStruct((m, LANE), F32),
                   jax.ShapeDtypeStruct((m, PLE_DIM), BF16)],
        compiler_params=_cparams(("parallel",)),
        name="post_norm",
    )(h, y, p, g_post.reshape(1, d), w_ple_bf)


def _shift_window(c_ref, nxt_ref, new_ref, o_ref, cache_chunks):
    step = pl.program_id(0) * pl.num_programs(1) + pl.program_id(1)
    is_last = step % cache_chunks == cache_chunks - 1
    t = c_ref.shape[0]
    o_ref[0:t - 1] = c_ref[1:t]
    o_ref[t - 1] = jnp.where(is_last, new_ref[...], nxt_ref[0])


def _shift_specs(cache, new, inner, steps):
    depth, nb, win = cache.shape[:3]
    assert (depth * nb * win) % steps == 0
    rows = depth * nb * win // steps
    assert win % rows == 0
    cpw = win // rows

    def where(a, b):
        s = a * inner + b
        return s // (nb * cpw), (s // cpw) % nb, s % cpw

    tail = cache.shape[3:]
    zeros = (0,) * len(tail)
    chunk = pl.BlockSpec((None, None, rows, *tail), lambda a, b: (*where(a, b), *zeros))
    nxt = pl.BlockSpec((None, None, 1, *tail),
                       lambda a, b: (*where(a, b)[:2], jnp.minimum((where(a, b)[2] + 1) * rows, win - 1), *zeros))
    newspec = pl.BlockSpec((None, None, *tail), lambda a, b: (*where(a, b)[:2], *zeros))
    return [chunk, nxt, newspec], [cache, cache, new], chunk, cpw


def _attn_kernel(*refs, seq, cache_chunks, first_layer):
    q_ref, k_ref, v_ref, g_ref, trow_ref = refs[:5]
    ni = 5 + (3 if cache_chunks else 0) + (0 if first_layer else 2)
    o_ref, ko_ref, vo_ref = refs[ni:ni + 3]
    kb_ref, vb_ref, s_ref, tab_ref = refs[-4:]
    bid = pl.program_id(1)
    if cache_chunks:
        _shift_window(*refs[5:8], refs[ni + 3], cache_chunks)

    @pl.when(bid == 0)
    def _():
        tab_ref[...] = pltpu.roll(jnp.broadcast_to(trow_ref[...], (QBLK, TAB_P)), 0, 1, stride=1, stride_axis=0)

    k = k_ref[...]
    v = v_ref[...]
    kb_ref[...] = k.astype(BF16)
    vb_ref[:, 0:HEAD_DIM] = v.astype(BF16)
    vb_ref[:, HEAD_DIM:2 * HEAD_DIM] = jnp.ones(v.shape, BF16)
    if first_layer:
        ko_ref[0], vo_ref[0] = k, v
        ko_ref[1] = jnp.zeros(k.shape, k.dtype)
        vo_ref[1] = jnp.zeros(v.shape, v.dtype)
    else:
        ko_ref[...] = k
        vo_ref[...] = v
    nblk = seq // QBLK
    for i in range(nblk):
        q = (q_ref[i * QBLK:(i + 1) * QBLK, :] * Q_SCALE).astype(BF16)
        nkeys = (i + 1) * QBLK
        tab0 = (nblk - i) * QBLK
        chunks = [(c, min(ATT_CHUNK, nkeys - c)) for c in range(0, nkeys, ATT_CHUNK)]
        mx = jnp.full((QBLK, LANE), NEG, F32)
        for c, w in chunks:
            s = lax.dot_general(q, kb_ref[c:c + w, :], (((1,), (1,)), ((), ())), preferred_element_type=F32)
            s = s + tab_ref[:, tab0 + c:tab0 + c + w]
            s_ref[i % ATT_SBUF, :, c:c + w] = s
            for t in range(0, w, LANE):
                mx = jnp.maximum(mx, s[:, t:t + LANE])
        m = jnp.max(mx, axis=-1, keepdims=True)
        acc = jnp.zeros((QBLK, 2 * HEAD_DIM), F32)
        for c, w in chunks:
            p = jnp.exp2(s_ref[i % ATT_SBUF, :, c:c + w] - m)
            acc = acc + jnp.dot(p.astype(BF16), vb_ref[c:c + w, :], preferred_element_type=F32)
        g = g_ref[i * QBLK:(i + 1) * QBLK, :]
        o = acc[:, 0:HEAD_DIM] * (1.0 / acc[:, HEAD_DIM:2 * HEAD_DIM])
        o_ref[i * QBLK:(i + 1) * QBLK, :] = (o * (g * jax.nn.sigmoid(g))).astype(o_ref.dtype)


def _attn_prompt(qkv, u4, trow, layer, batch, seq, stacks=None, cache=None):
    m = qkv.shape[0]
    first = stacks is None
    assert first == (layer == 0) and DEPTH == 2
    blk = lambda h, b: (b, h)
    col = lambda part: pl.BlockSpec((seq, HEAD_DIM), lambda h, b: (b, part * N_HEADS + h))
    args = [qkv, qkv, qkv, u4, trow]
    in_specs = [col(0), col(1), col(2), pl.BlockSpec((seq, HEAD_DIM), blk),
                pl.BlockSpec((None, 1, TAB_P), lambda h, b: (h, 0, 0))]
    stack_shape = jax.ShapeDtypeStruct((DEPTH, m, ATT_W), F32)
    if first:
        stack_spec = pl.BlockSpec((DEPTH, seq, HEAD_DIM), lambda h, b: (0, b, h))
    else:
        stack_spec = pl.BlockSpec((None, seq, HEAD_DIM), lambda h, b: (layer, b, h))
    out_specs = [pl.BlockSpec((seq, HEAD_DIM), blk), stack_spec, stack_spec]
    out_shape = [jax.ShapeDtypeStruct((m, ATT_W), BF16), stack_shape, stack_shape]
    cpw = 0
    if cache is not None:
        specs, cargs, ospec, cpw = _shift_specs(*cache, inner=batch, steps=N_HEADS * batch)
        args += cargs
        in_specs += specs
        out_specs.append(ospec)
        out_shape.append(jax.ShapeDtypeStruct(cache[0].shape, cache[0].dtype))
    aliases = {}
    if not first:
        aliases = {len(args): 1, len(args) + 1: 2}
        args += list(stacks)
        in_specs += [pl.BlockSpec(memory_space=pl.ANY)] * 2
    return pl.pallas_call(
        functools.partial(_attn_kernel, seq=seq, cache_chunks=cpw, first_layer=first),
        input_output_aliases=aliases,
        grid=(N_HEADS, batch),
        in_specs=in_specs,
        out_specs=out_specs,
        out_shape=out_shape,
        scratch_shapes=[pltpu.VMEM((seq, HEAD_DIM), BF16), pltpu.VMEM((seq, 2 * HEAD_DIM), BF16),
                        pltpu.VMEM((ATT_SBUF, QBLK, seq), F32), pltpu.VMEM((QBLK, TAB_P), F32)],
        compiler_params=_cparams(("arbitrary", "arbitrary")),
        name="attn_prompt",
    )(*args)


CONV_TB = 256
CONV_HIST = 32
CONV_RC = 32
NCH = CONV_C // LANE


def _conv_kernel(ga_ref, gb_ref, gate_ref, w_ref, cb_ref, lg_ref, lb_ref, yb_ref, st_ref, ucat_ref, y_ref):
    t = pl.program_id(1)
    tb = CONV_TB

    @pl.when(t == 0)
    def _():
        ucat_ref[:, 0:CONV_HIST, :] = jnp.zeros((NCH, CONV_HIST, LANE), F32)

    @pl.when(t > 0)
    def _():
        ucat_ref[:, 0:CONV_HIST, :] = ucat_ref[:, tb:tb + CONV_HIST, :]

    for c in range(NCH):
        sl = slice(c * LANE, (c + 1) * LANE)
        ga = ga_ref[:, sl]
        ucat_ref[c, CONV_HIST:CONV_HIST + tb, :] = ga * _sigmoid(gb_ref[:, sl])

    off = CONV_HIST - (CONV_K - 1)

    def chunk(c, carry):
        wc = w_ref[c]
        bias = cb_ref[c]
        for r in range(0, tb, CONV_RC):
            acc = jnp.broadcast_to(bias, (CONV_RC, LANE))
            for k in range(CONV_K):
                acc = acc + wc[k:k + 1, :] * ucat_ref[c, r + off + k:r + off + k + CONV_RC, :]
            y_ref[c, r:r + CONV_RC, :] = acc
        return carry

    lax.fori_loop(0, NCH, chunk, 0)

    ssum = jnp.zeros((tb, LANE), F32)
    for c in range(NCH):
        ssum = ssum + y_ref[c]
    mu = jnp.sum(ssum, axis=-1, keepdims=True) * (1.0 / CONV_C)
    sq = jnp.zeros((tb, LANE), F32)
    for c in range(NCH):
        d = y_ref[c] - mu
        sq = sq + d * d
    rstd = lax.rsqrt(jnp.sum(sq, axis=-1, keepdims=True) * (1.0 / CONV_C) + EPS)
    for c in range(NCH):
        sl = slice(c * LANE, (c + 1) * LANE)
        z = (y_ref[c] - mu) * rstd * lg_ref[c] + lb_ref[c]
        g = gate_ref[:, sl]
        yb_ref[:, sl] = (z * _sigmoid(z) * (g * _sigmoid(g))).astype(yb_ref.dtype)

    @pl.when(t == pl.num_programs(1) - 1)
    def _():
        for c in range(NCH):
            st_ref[:, c * LANE:(c + 1) * LANE] = ucat_ref[c, CONV_HIST + tb - (CONV_K - 1):CONV_HIST + tb, :]


def _chunked(v):
    return v.reshape(v.shape[0], NCH, LANE).transpose(1, 0, 2)


def _conv_prompt(u4, conv_w, conv_b, ln_g, ln_b, batch, seq):
    m = u4.shape[0]
    nt = seq // CONV_TB
    wpad = jnp.zeros((CONV_HIST, CONV_C), F32).at[:CONV_K].set(conv_w)
    col = lambda j: pl.BlockSpec((CONV_TB, CONV_C), lambda b, t, j=j: (b * nt + t, j))
    vec = pl.BlockSpec((NCH, 1, LANE), lambda b, t: (0, 0, 0))
    return pl.pallas_call(
        _conv_kernel,
        grid=(batch, nt),
        in_specs=[col(1), col(2), col(3), pl.BlockSpec((NCH, CONV_HIST, LANE), lambda b, t: (0, 0, 0)), vec, vec, vec],
        out_specs=[pl.BlockSpec((CONV_TB, CONV_C), lambda b, t: (b * nt + t, 0)),
                   pl.BlockSpec((None, CONV_K - 1, CONV_C), lambda b, t: (b, 0, 0))],
        out_shape=[jax.ShapeDtypeStruct((m, CONV_C), BF16),
                   jax.ShapeDtypeStruct((batch, CONV_K - 1, CONV_C), F32)],
        scratch_shapes=[pltpu.VMEM((NCH, CONV_HIST + CONV_TB, LANE), F32), pltpu.VMEM((NCH, CONV_TB, LANE), F32)],
        compiler_params=_cparams(("parallel", "arbitrary")),
        name="conv_prompt",
    )(u4, u4, u4, _chunked(wpad), _chunked(conv_b.reshape(1, -1)), _chunked(ln_g.reshape(1, -1)),
      _chunked(ln_b.reshape(1, -1)))


def _conv_s_kernel(u4_ref, st_ref, w_ref, cb_ref, lg_ref, lb_ref, yb_ref, so_ref, *, nb):
    ga = u4_ref[:, CONV_C:2 * CONV_C]
    u = ga * jax.nn.sigmoid(u4_ref[:, 2 * CONV_C:3 * CONV_C])
    gate = u4_ref[:, 3 * CONV_C:4 * CONV_C]
    yb_ref[...] = jnp.zeros(yb_ref.shape, yb_ref.dtype)
    for b in range(nb):
        st = st_ref[b]
        un = u[b:b + 1, :]
        y = (jnp.sum(st * w_ref[0:CONV_K - 1, :], axis=0, keepdims=True)
             + un * w_ref[CONV_K - 1:CONV_K, :] + cb_ref[...])
        mu = jnp.mean(y, axis=-1, keepdims=True)
        d = y - mu
        z = d * lax.rsqrt(jnp.mean(d * d, axis=-1, keepdims=True) + EPS) * lg_ref[...] + lb_ref[...]
        g = gate[b:b + 1, :]
        yb_ref[b:b + 1, :] = (z * jax.nn.sigmoid(z) * (g * jax.nn.sigmoid(g))).astype(yb_ref.dtype)
        so_ref[b, 0:CONV_K - 2, :] = st[1:CONV_K - 1, :]
        so_ref[b, CONV_K - 2:CONV_K - 1, :] = un


def _conv_sample(u4s, state, conv_w, conv_b, ln_g, ln_b):
    nb = state.shape[0]
    return pl.pallas_call(
        functools.partial(_conv_s_kernel, nb=nb),
        out_shape=[jax.ShapeDtypeStruct((SAMPLE_ROWS, CONV_C), F32),
                   jax.ShapeDtypeStruct(state.shape, F32)],
        compiler_params=pltpu.CompilerParams(vmem_limit_bytes=VMEM_LIMIT),
        name="conv_sample",
    )(u4s, state, conv_w, conv_b.reshape(1, -1), ln_g.reshape(1, -1), ln_b.reshape(1, -1))


def _attn_s_kernel(q_ref, kn_ref, vn_ref, g_ref, *refs):
    npat = len(DILATED_PAIRS)
    k_refs, v_refs = refs[:npat], refs[npat:2 * npat]
    tab_ref, tnew_ref, o_ref = refs[2 * npat:]
    q = q_ref[...]
    s = [jnp.sum(kr[...] * q[None], axis=-1, keepdims=True) + tab_ref[p]
         for p, kr in enumerate(k_refs)]
    s_new = jnp.sum(q * kn_ref[...], axis=-1, keepdims=True) + tnew_ref[...]
    m = s_new
    for sp in s:
        m = jnp.maximum(m, jnp.max(sp, axis=0))
    p_new = jnp.exp2(s_new - m)
    l = p_new
    acc = p_new * vn_ref[...]
    for sp, vr in zip(s, v_refs):
        p = jnp.exp2(sp - m[None])
        l = l + jnp.sum(p, axis=0)
        acc = acc + jnp.sum(p * vr[...], axis=0)
    g = g_ref[...]
    o_ref[...] = acc * (1.0 / l) * (g * jax.nn.sigmoid(g))


def _attn_sample(qs, kn, vn, gs, cache_k, cache_v, tab_s, tab_new, layer):
    nb = qs.shape[0]
    rows = PATTERN_KEYS
    row = pl.BlockSpec((None, N_HEADS, HEAD_DIM), lambda b: (b, 0, 0))
    views, specs = [], []
    for w, d in DILATED_PAIRS:
        assert (WIN_MAX - w) % (d * rows) == 0
        views.append(lambda c, d=d: c.reshape(DEPTH, nb, WIN_MAX // d, d, N_HEADS, HEAD_DIM))
        specs.append(pl.BlockSpec((None, None, rows, None, N_HEADS, HEAD_DIM),
                                  lambda b, blk=(WIN_MAX - w) // (d * rows): (layer, b, blk, 0, 0, 0)))
    npat = len(DILATED_PAIRS)
    return pl.pallas_call(
        _attn_s_kernel,
        grid=(nb,),
        in_specs=[row, row, row, row, *specs, *specs,
                  pl.BlockSpec((npat, rows, N_HEADS, 1), lambda b: (0, 0, 0, 0)),
                  pl.BlockSpec((N_HEADS, 1), lambda b: (0, 0))],
        out_specs=row,
        out_shape=jax.ShapeDtypeStruct((nb, N_HEADS, HEAD_DIM), F32),
        compiler_params=_cparams(("parallel",)),
        name="attn_sample",
    )(qs, kn, vn, gs, *[v(cache_k) for v in views], *[v(cache_v) for v in views], tab_s, tab_new)


def _rel_bucket(dist):
    max_exact = N_BUCKETS // 2
    df = jnp.maximum(dist, 1).astype(F32)
    large = max_exact + (jnp.log(df / max_exact) / math.log(MAX_DISTANCE / max_exact)
                         * (N_BUCKETS - max_exact)).astype(jnp.int32)
    large = jnp.minimum(large, N_BUCKETS - 1)
    return jnp.where(dist < max_exact, dist, large)


def _bias_tables(rel_bias):
    dist = jnp.arange(WIN_MAX + 1, dtype=jnp.int32)
    bias = rel_bias[_rel_bucket(dist)].astype(F32).T
    mult = sum(((dist % d == 0) & (dist <= w)).astype(F32) for w, d in DILATED_PAIRS)
    t1d = jnp.where(mult > 0, (bias + jnp.log(jnp.maximum(mult, 1.0))) * LOG2E, NEG)
    trow = jnp.full((N_HEADS, 1, TAB_P), NEG, F32).at[:, 0, :WIN_MAX + 1].set(t1d[:, ::-1])
    j = jnp.arange(PATTERN_KEYS, dtype=jnp.int32)
    d_rows = jnp.stack([w - d * j for w, d in DILATED_PAIRS])
    tab_s = (bias[:, d_rows] * LOG2E).transpose(1, 2, 0)[..., None]
    tab_new = ((bias[:, 0] + math.log(len(DILATED_PAIRS))) * LOG2E).reshape(N_HEADS, 1)
    return trow, tab_s, tab_new


def kernel(x_prompt, x_sample, cache_k, cache_v, state_conv, p_prompt, p_sample, rel_bias, g_pre, w_in, conv_w,
           conv_b, ln_g, ln_b, w_out, g_post, w_ple, g_ple, w_pg, b_pg):
    batch, seq, d = x_prompt.shape
    nb = x_sample.shape[0]
    m = batch * seq
    pad = lambda a: jnp.zeros((SAMPLE_ROWS, a.size // nb), a.dtype).at[:nb].set(a.reshape(nb, -1))
    heads = lambda a: a[:nb, :ATT_W].reshape(nb, N_HEADS, HEAD_DIM)

    hp = x_prompt.reshape(m, d)
    hs = pad(x_sample)
    trow, tab_s, tab_new = _bias_tables(rel_bias)
    p_all = p_prompt.reshape(DEPTH, m, PLE_DIM)
    ps_all = jnp.stack([pad(p_sample[l]) for l in range(DEPTH)])
    wple_bf = w_ple.astype(BF16)
    stacks = new_k = new_v = None
    cp_l, cs_l, kn_l, vn_l = [], [], [], []
    for l in range(DEPTH):
        xn = _rmsnorm(hp, g_pre[l], 512)
        xns = _rmsnorm(hs, g_pre[l], SAMPLE_ROWS)
        qkv, qkvs = _mm([xn], [xns], w_in, l, 0, 3 * ATT_W, name="proj_qkv")
        qs = heads(qkvs) * Q_SCALE
        kn_l.append(heads(qkvs[:, ATT_W:]))
        vn_l.append(heads(qkvs[:, 2 * ATT_W:]))
        if l == DEPTH - 1:
            u4, u4s, new_v = _mm([xn], [xns], w_in, l, 3 * ATT_W, ATT_W + 3 * CONV_C, name="proj_rest",
                                 cache=(cache_v, jnp.stack(vn_l)))
            ya, *stacks, new_k = _attn_prompt(qkv, u4, trow, l, batch, seq, stacks=stacks,
                                              cache=(cache_k, jnp.stack(kn_l)))
        else:
            u4, u4s = _mm([xn], [xns], w_in, l, 3 * ATT_W, ATT_W + 3 * CONV_C, name="proj_rest")
            ya, *stacks = _attn_prompt(qkv, u4, trow, l, batch, seq, stacks=stacks)
        yb, cstate = _conv_prompt(u4, conv_w[l], conv_b[l], ln_g[l], ln_b[l], batch, seq)

        yas = _attn_sample(qs, kn_l[l], vn_l[l], heads(u4s), cache_k, cache_v, tab_s, tab_new, l)
        ybs, cstate_s = _conv_sample(u4s, state_conv[l], conv_w[l], conv_b[l], ln_g[l], ln_b[l])

        y, ys = _mm([ya, yb], [pad(yas).astype(BF16), ybs.astype(BF16)], w_out, l, 0, d, name="proj_out")
        h1b, rs, pn = _post(hp, y, p_all, l, g_post[l], wple_bf[l], 512)
        h1bs, rss, pns = _post(hs, ys, ps_all, l, g_post[l], wple_bf[l], SAMPLE_ROWS)
        hp, hs = _gate_mm(h1b, h1bs, w_pg, b_pg, l, g_post[l], wple_bf[l], g_ple[l],
                          (hp, y, rs, pn), (hs, ys, rss, pns))
        cp_l.append(cstate)
        cs_l.append(cstate_s)

    kv_shape = (DEPTH, batch, seq, N_HEADS, HEAD_DIM)
    return (hp.reshape(batch, seq, d), hs[:nb].reshape(nb, 1, d),
            stacks[0].reshape(kv_shape), stacks[1].reshape(kv_shape), jnp.stack(cp_l),
            new_k, new_v, jnp.stack(cs_l))
```

```python
import functools
import math

import jax
import jax.numpy as jnp
from jax import lax
from jax.experimental import pallas as pl
from jax.experimental.pallas import tpu as pltpu

D_MODEL = 4096
HEAD_DIM = 128
ATT_W = 2048
N_HEADS = 16
CONV_C = 2048
CONV_K = 31
PLE_DIM = 256
N_BUCKETS = 32
MAX_DISTANCE = 2048
DILATED_PAIRS = ((128, 1), (512, 4), (2048, 16))
WIN_MAX = max(w for w, _ in DILATED_PAIRS)
PATTERN_KEYS = DILATED_PAIRS[0][0] // DILATED_PAIRS[0][1]
assert all(w // d == PATTERN_KEYS for w, d in DILATED_PAIRS)
EPS = 1e-6
NEG = -1e30
LOG2E = math.log2(math.e)
Q_SCALE = HEAD_DIM ** -0.5 * LOG2E
DEPTH = 2

F32 = jnp.float32
BF16 = jnp.bfloat16

VMEM_LIMIT = 62 * 1024 * 1024
SAMPLE_ROWS = 16
LANE = 128
QBLK = 256
ATT_CHUNK = 256
TAB_W = WIN_MAX + QBLK
TAB_P = TAB_W + QBLK


def _sigmoid(x):
    return 0.5 * jnp.tanh(0.5 * x) + 0.5


def _cparams(sem):
    return pltpu.CompilerParams(dimension_semantics=sem, vmem_limit_bytes=VMEM_LIMIT)


def _rmsnorm_kernel(x_ref, g_ref, o_ref):
    x = x_ref[...]
    r = x * lax.rsqrt(jnp.mean(x * x, axis=-1, keepdims=True) + EPS)
    o_ref[...] = (r * g_ref[...]).astype(o_ref.dtype)


def _rmsnorm(x, g, tm):
    m, d = x.shape
    return pl.pallas_call(
        _rmsnorm_kernel,
        grid=(m // tm,),
        in_specs=[pl.BlockSpec((tm, d), lambda i: (i, 0)), pl.BlockSpec((1, d), lambda i: (0, 0))],
        out_specs=pl.BlockSpec((tm, d), lambda i: (i, 0)),
        out_shape=jax.ShapeDtypeStruct((m, d), BF16),
        compiler_params=_cparams(("parallel",)),
        name="rmsnorm",
    )(x, g.reshape(1, d))


MM_TM = 2048
MM_TN = 256
MM_RB = 256


def _sample_spec(ms, tn):
    return pl.BlockSpec((None, ms, tn), lambda i, j: (i, 0, j))


def _mm_kernel(*refs, nparts, cache_chunks):
    xs = refs[:nparts]
    ss = refs[nparts:2 * nparts]
    ws = refs[2 * nparts:3 * nparts]
    no = 3 * nparts
    shift_refs = None
    if cache_chunks:
        shift_refs = (*refs[no:no + 3], refs[no + 5])
        no += 3
    o_ref, os_ref = refs[no:no + 2]
    wbs = refs[no + (3 if cache_chunks else 2):]
    for w_ref, wb_ref in zip(ws, wbs):
        wb_ref[...] = w_ref[...].astype(BF16)

    def product(parts, rows):
        acc = None
        for x_ref, wb_ref in zip(parts, wbs):
            d = jnp.dot(x_ref[rows, :], wb_ref[...], preferred_element_type=F32)
            acc = d if acc is None else acc + d
        return acc

    tm = xs[0].shape[0]
    for r in range(0, tm, min(tm, MM_RB)):
        rows = slice(r, r + min(tm, MM_RB))
        o_ref[rows, :] = product(xs, rows).astype(o_ref.dtype)
    if shift_refs is not None:
        _shift_window(*shift_refs, cache_chunks)

    @pl.when(pl.program_id(0) == 0)
    def _():
        os_ref[...] = product(ss, slice(None)).astype(os_ref.dtype)

    @pl.when(pl.program_id(0) > 0)
    def _():
        os_ref[...] = jnp.zeros(os_ref.shape, os_ref.dtype)


def _mm(x_parts, s_parts, w, layer, col0, ncols, *, out_dtype=F32, tm=MM_TM, tn=MM_TN, name="mm", cache=None):
    nparts = len(x_parts)
    m = x_parts[0].shape[0]
    ms = s_parts[0].shape[0]
    kparts = [x.shape[1] for x in x_parts]
    assert sum(kparts) == w.shape[1] and col0 % tn == 0 and ncols % tn == 0 and m % tm == 0
    jb0 = col0 // tn
    koff = [sum(kparts[:p]) for p in range(nparts)]
    in_specs = (
        [pl.BlockSpec((tm, kp), lambda i, j: (i, 0)) for kp in kparts]
        + [pl.BlockSpec((ms, kp), lambda i, j: (0, 0)) for kp in kparts]
        + [pl.BlockSpec((None, kp, tn), functools.partial(lambda i, j, kb: (layer, kb, jb0 + j), kb=ko // kp))
           for kp, ko in zip(kparts, koff)]
    )
    args = [*x_parts, *s_parts, *([w] * nparts)]
    out_specs = [pl.BlockSpec((tm, tn), lambda i, j: (i, j)), _sample_spec(ms, tn)]
    out_shape = [jax.ShapeDtypeStruct((m, ncols), out_dtype),
                 jax.ShapeDtypeStruct((m // tm, ms, ncols), out_dtype)]
    cpw = 0
    if cache is not None:
        specs, cargs, ospec, cpw = _shift_specs(*cache, inner=ncols // tn, steps=(m // tm) * (ncols // tn))
        args += cargs
        in_specs += specs
        out_specs.append(ospec)
        out_shape.append(jax.ShapeDtypeStruct(cache[0].shape, cache[0].dtype))
    out, out_s, *shifted = pl.pallas_call(
        functools.partial(_mm_kernel, nparts=nparts, cache_chunks=cpw),
        grid=(m // tm, ncols // tn),
        in_specs=in_specs,
        out_specs=out_specs,
        out_shape=out_shape,
        scratch_shapes=[pltpu.VMEM((kp, tn), BF16) for kp in kparts],
        compiler_params=_cparams(("arbitrary", "arbitrary")),
        name=name,
    )(*args)
    return (out, out_s[0], *shifted)


def _gate_update(x_ref, wb_ref, b_ref, gpost_ref, wple_ref, gple_ref, h_ref, y_ref, rs_ref, pn_ref, rows):
    acc = jnp.dot(x_ref[rows, :], wb_ref[...], preferred_element_type=F32)
    pen = jnp.dot(pn_ref[rows, :], wple_ref[...], preferred_element_type=F32) * gple_ref[...]
    rs = jnp.concatenate([rs_ref[rows, :]] * (acc.shape[1] // LANE), axis=1)
    h1 = h_ref[rows, :] + y_ref[rows, :] * rs * gpost_ref[...]
    return h1 + jax.nn.sigmoid(acc + b_ref[...]) * pen


def _gate_kernel(x_ref, s_ref, w_ref, b_ref, gpost_ref, wple_ref, gple_ref, h_ref, y_ref, rs_ref, pn_ref,
                 hs_ref, ys_ref, rss_ref, pns_ref, o_ref, os_ref, wb_ref):
    wb_ref[...] = w_ref[...].astype(BF16)
    shared = (wb_ref, b_ref, gpost_ref, wple_ref, gple_ref)
    tm = x_ref.shape[0]
    for r in range(0, tm, min(tm, MM_RB)):
        rows = slice(r, r + min(tm, MM_RB))
        o_ref[rows, :] = _gate_update(x_ref, *shared, h_ref, y_ref, rs_ref, pn_ref, rows)

    @pl.when(pl.program_id(0) == 0)
    def _():
        os_ref[...] = _gate_update(s_ref, *shared, hs_ref, ys_ref, rss_ref, pns_ref, slice(None))

    @pl.when(pl.program_id(0) > 0)
    def _():
        os_ref[...] = jnp.zeros(os_ref.shape, os_ref.dtype)


def _gate_mm(x, s, w, b, layer, g_post, w_ple_bf, g_ple, prompt, sample, *, tm=MM_TM, tn=MM_TN):
    m, k = x.shape
    ms = s.shape[0]
    n = w.shape[2]
    vec = pl.BlockSpec((1, tn), lambda i, j: (0, j))

    def group(rows, row_block):
        tile = pl.BlockSpec((rows, tn), lambda i, j: (row_block(i), j))
        return [tile, tile, pl.BlockSpec((rows, LANE), lambda i, j: (row_block(i), 0)),
                pl.BlockSpec((rows, PLE_DIM), lambda i, j: (row_block(i), 0))]

    out, out_s = pl.pallas_call(
        _gate_kernel,
        grid=(m // tm, n // tn),
        in_specs=[pl.BlockSpec((tm, k), lambda i, j: (i, 0)), pl.BlockSpec((ms, k), lambda i, j: (0, 0)),
                  pl.BlockSpec((None, k, tn), lambda i, j: (layer, 0, j)),
                  pl.BlockSpec((None, 1, tn), lambda i, j: (layer, 0, j)),
                  vec, pl.BlockSpec((PLE_DIM, tn), lambda i, j: (0, j)), vec,
                  *group(tm, lambda i: i), *group(ms, lambda i: 0)],
        out_specs=[pl.BlockSpec((tm, tn), lambda i, j: (i, j)), _sample_spec(ms, tn)],
        out_shape=[jax.ShapeDtypeStruct((m, n), F32), jax.ShapeDtypeStruct((m // tm, ms, n), F32)],
        scratch_shapes=[pltpu.VMEM((k, tn), BF16)],
        compiler_params=_cparams(("arbitrary", "arbitrary")),
        name="gate_mm",
    )(x, s, w, b.reshape(DEPTH, 1, n), g_post.reshape(1, n), w_ple_bf, g_ple.reshape(1, n), *prompt, *sample)
    return out, out_s[0]


def _post_kernel(h_ref, y_ref, p_ref, gpost_ref, wple_ref, h1b_ref, rs_ref, pn_ref):
    y = y_ref[...]
    rs_y = lax.rsqrt(jnp.mean(y * y, axis=-1, keepdims=True) + EPS)
    h1b_ref[...] = (h_ref[...] + y * rs_y * gpost_ref[...]).astype(BF16)
    rs_ref[...] = jnp.broadcast_to(rs_y, rs_ref.shape)
    p = p_ref[...]
    pe = jnp.dot(p.astype(BF16), wple_ref[...], preferred_element_type=F32)
    rs = lax.rsqrt(jnp.mean(pe * pe, axis=-1, keepdims=True) + EPS)
    pn_ref[...] = (p * rs).astype(BF16)


def _post(h, y, p, layer, g_post, w_ple_bf, tm):
    m, d = h.shape
    row = pl.BlockSpec((tm, d), lambda i: (i, 0))
    vec = pl.BlockSpec((1, d), lambda i: (0, 0))
    return pl.pallas_call(
        _post_kernel,
        grid=(m // tm,),
        in_specs=[row, row, pl.BlockSpec((None, tm, PLE_DIM), lambda i: (layer, i, 0)), vec,
                  pl.BlockSpec((PLE_DIM, d), lambda i: (0, 0))],
        out_specs=[row, pl.BlockSpec((tm, LANE), lambda i: (i, 0)), pl.BlockSpec((tm, PLE_DIM), lambda i: (i, 0))],
        out_shape=[jax.ShapeDtypeStruct((m, d), BF16), jax.ShapeDtypeStruct((m, LANE), F32),
                   jax.ShapeDtypeStruct((m, PLE_DIM), BF16)],
        compiler_params=_cparams(("parallel",)),
        name="post_norm",
    )(h, y, p, g_post.reshape(1, d), w_ple_bf)


def _shift_window(c_ref, nxt_ref, new_ref, o_ref, cache_chunks):
    step = pl.program_id(0) * pl.num_programs(1) + pl.program_id(1)
    is_last = step % cache_chunks == cache_chunks - 1
    t = c_ref.shape[0]
    o_ref[0:t - 1] = c_ref[1:t]
    o_ref[t - 1] = jnp.where(is_last, new_ref[...], nxt_ref[0])


def _shift_specs(cache, new, inner, steps):
    depth, nb, win = cache.shape[:3]
    assert (depth * nb * win) % steps == 0
    rows = depth * nb * win // steps
    assert win % rows == 0
    cpw = win // rows

    def where(a, b):
        s = a * inner + b
        return s // (nb * cpw), (s // cpw) % nb, s % cpw

    tail = cache.shape[3:]
    zeros = (0,) * len(tail)
    chunk = pl.BlockSpec((None, None, rows, *tail), lambda a, b: (*where(a, b), *zeros))
    nxt = pl.BlockSpec((None, None, 1, *tail),
                       lambda a, b: (*where(a, b)[:2], jnp.minimum((where(a, b)[2] + 1) * rows, win - 1), *zeros))
    newspec = pl.BlockSpec((None, None, *tail), lambda a, b: (*where(a, b)[:2], *zeros))
    return [chunk, nxt, newspec], [cache, cache, new], chunk, cpw


def _attn_kernel(*refs, seq, cache_chunks, first_layer):
    q_ref, k_ref, v_ref, g_ref, trow_ref = refs[:5]
    ni = 5 + (3 if cache_chunks else 0) + (0 if first_layer else 2)
    o_ref, ko_ref, vo_ref = refs[ni:ni + 3]
    kb_ref, vb_ref, tab_ref = refs[-3:]
    bid = pl.program_id(1)
    if cache_chunks:
        _shift_window(*refs[5:8], refs[ni + 3], cache_chunks)

    @pl.when(bid == 0)
    def _():
        tab_ref[...] = pltpu.roll(jnp.broadcast_to(trow_ref[...], (QBLK, TAB_P)), 0, 1, stride=1, stride_axis=0)

    nblk = seq // QBLK
    for i in range(nblk):
        rows = slice(i * QBLK, (i + 1) * QBLK)
        k = k_ref[rows, :]
        v = v_ref[rows, :]
        kb_ref[rows, :] = k.astype(BF16)
        vb_ref[rows, 0:HEAD_DIM] = v.astype(BF16)
        vb_ref[rows, HEAD_DIM:2 * HEAD_DIM] = jnp.ones(v.shape, BF16)
        if first_layer:
            ko_ref[0, rows, :], vo_ref[0, rows, :] = k, v
            ko_ref[1, rows, :] = jnp.zeros(k.shape, k.dtype)
            vo_ref[1, rows, :] = jnp.zeros(v.shape, v.dtype)
        else:
            ko_ref[rows, :] = k
            vo_ref[rows, :] = v
        q = (q_ref[rows, :] * Q_SCALE).astype(BF16)
        nkeys = (i + 1) * QBLK
        tab0 = (nblk - i) * QBLK
        chunks = [(c, min(ATT_CHUNK, nkeys - c)) for c in range(0, nkeys, ATT_CHUNK)]
        m = jnp.full((QBLK, 1), NEG, F32)
        acc = jnp.zeros((QBLK, 2 * HEAD_DIM), F32)
        for c, w in chunks:
            s = lax.dot_general(q, kb_ref[c:c + w, :], (((1,), (1,)), ((), ())), preferred_element_type=F32)
            s = s + tab_ref[:, tab0 + c:tab0 + c + w]
            mx = s[:, 0:LANE]
            for t in range(LANE, w, LANE):
                mx = jnp.maximum(mx, s[:, t:t + LANE])
            m_new = jnp.maximum(m, jnp.max(mx, axis=-1, keepdims=True))
            p = jnp.exp2(s - m_new)
            acc = acc * jnp.exp2(m - m_new) + jnp.dot(p.astype(BF16), vb_ref[c:c + w, :], preferred_element_type=F32)
            m = m_new
        g = g_ref[i * QBLK:(i + 1) * QBLK, :]
        o = acc[:, 0:HEAD_DIM] * (1.0 / acc[:, HEAD_DIM:2 * HEAD_DIM])
        o_ref[i * QBLK:(i + 1) * QBLK, :] = (o * (g * jax.nn.sigmoid(g))).astype(o_ref.dtype)


def _attn_prompt(qkv, u4, trow, layer, batch, seq, stacks=None, cache=None):
    m = qkv.shape[0]
    first = stacks is None
    assert first == (layer == 0) and DEPTH == 2
    blk = lambda h, b: (b, h)
    col = lambda part: pl.BlockSpec((seq, HEAD_DIM), lambda h, b: (b, part * N_HEADS + h))
    args = [qkv, qkv, qkv, u4, trow]
    in_specs = [col(0), col(1), col(2), pl.BlockSpec((seq, HEAD_DIM), blk),
                pl.BlockSpec((None, 1, TAB_P), lambda h, b: (h, 0, 0))]
    stack_shape = jax.ShapeDtypeStruct((DEPTH, m, ATT_W), F32)
    if first:
        stack_spec = pl.BlockSpec((DEPTH, seq, HEAD_DIM), lambda h, b: (0, b, h))
    else:
        stack_spec = pl.BlockSpec((None, seq, HEAD_DIM), lambda h, b: (layer, b, h))
    out_specs = [pl.BlockSpec((seq, HEAD_DIM), blk), stack_spec, stack_spec]
    out_shape = [jax.ShapeDtypeStruct((m, ATT_W), BF16), stack_shape, stack_shape]
    cpw = 0
    if cache is not None:
        specs, cargs, ospec, cpw = _shift_specs(*cache, inner=batch, steps=N_HEADS * batch)
        args += cargs
        in_specs += specs
        out_specs.append(ospec)
        out_shape.append(jax.ShapeDtypeStruct(cache[0].shape, cache[0].dtype))
    aliases = {}
    if not first:
        aliases = {len(args): 1, len(args) + 1: 2}
        args += list(stacks)
        in_specs += [pl.BlockSpec(memory_space=pl.ANY)] * 2
    return pl.pallas_call(
        functools.partial(_attn_kernel, seq=seq, cache_chunks=cpw, first_layer=first),
        input_output_aliases=aliases,
        grid=(N_HEADS, batch),
        in_specs=in_specs,
        out_specs=out_specs,
        out_shape=out_shape,
        scratch_shapes=[pltpu.VMEM((seq, HEAD_DIM), BF16), pltpu.VMEM((seq, 2 * HEAD_DIM), BF16),
                        pltpu.VMEM((QBLK, TAB_P), F32)],
        compiler_params=_cparams(("arbitrary", "arbitrary")),
        name="attn_prompt",
    )(*args)


CONV_TB = 256
CONV_HIST = 32
CONV_RC = 32
NCH = CONV_C // LANE


def _conv_kernel(ga_ref, gb_ref, gate_ref, w_ref, cb_ref, lg_ref, lb_ref, yb_ref, st_ref, ucat_ref, y_ref):
    t = pl.program_id(1)
    tb = CONV_TB

    @pl.when(t == 0)
    def _():
        ucat_ref[:, 0:CONV_HIST, :] = jnp.zeros((NCH, CONV_HIST, LANE), F32)

    @pl.when(t > 0)
    def _():
        ucat_ref[:, 0:CONV_HIST, :] = ucat_ref[:, tb:tb + CONV_HIST, :]

    for c in range(NCH):
        sl = slice(c * LANE, (c + 1) * LANE)
        ga = ga_ref[:, sl]
        ucat_ref[c, CONV_HIST:CONV_HIST + tb, :] = ga * _sigmoid(gb_ref[:, sl])

    off = CONV_HIST - (CONV_K - 1)

    def chunk(c, carry):
        wc = w_ref[c]
        bias = cb_ref[c]
        for r in range(0, tb, CONV_RC):
            acc = jnp.broadcast_to(bias, (CONV_RC, LANE))
            for k in range(CONV_K):
                acc = acc + wc[k:k + 1, :] * ucat_ref[c, r + off + k:r + off + k + CONV_RC, :]
            y_ref[c, r:r + CONV_RC, :] = acc
        return carry

    lax.fori_loop(0, NCH, chunk, 0)

    ssum = jnp.zeros((tb, LANE), F32)
    for c in range(NCH):
        ssum = ssum + y_ref[c]
    mu = jnp.sum(ssum, axis=-1, keepdims=True) * (1.0 / CONV_C)
    sq = jnp.zeros((tb, LANE), F32)
    for c in range(NCH):
        d = y_ref[c] - mu
        sq = sq + d * d
    rstd = lax.rsqrt(jnp.sum(sq, axis=-1, keepdims=True) * (1.0 / CONV_C) + EPS)
    for c in range(NCH):
        sl = slice(c * LANE, (c + 1) * LANE)
        z = (y_ref[c] - mu) * rstd * lg_ref[c] + lb_ref[c]
        g = gate_ref[:, sl]
        yb_ref[:, sl] = (z * _sigmoid(z) * (g * _sigmoid(g))).astype(yb_ref.dtype)

    @pl.when(t == pl.num_programs(1) - 1)
    def _():
        for c in range(NCH):
            st_ref[:, c * LANE:(c + 1) * LANE] = ucat_ref[c, CONV_HIST + tb - (CONV_K - 1):CONV_HIST + tb, :]


def _chunked(v):
    return v.reshape(v.shape[0], NCH, LANE).transpose(1, 0, 2)


def _conv_prompt(u4, conv_w, conv_b, ln_g, ln_b, batch, seq):
    m = u4.shape[0]
    nt = seq // CONV_TB
    wpad = jnp.zeros((CONV_HIST, CONV_C), F32).at[:CONV_K].set(conv_w)
    col = lambda j: pl.BlockSpec((CONV_TB, CONV_C), lambda b, t, j=j: (b * nt + t, j))
    vec = pl.BlockSpec((NCH, 1, LANE), lambda b, t: (0, 0, 0))
    return pl.pallas_call(
        _conv_kernel,
        grid=(batch, nt),
        in_specs=[col(1), col(2), col(3), pl.BlockSpec((NCH, CONV_HIST, LANE), lambda b, t: (0, 0, 0)), vec, vec, vec],
        out_specs=[pl.BlockSpec((CONV_TB, CONV_C), lambda b, t: (b * nt + t, 0)),
                   pl.BlockSpec((None, CONV_K - 1, CONV_C), lambda b, t: (b, 0, 0))],
        out_shape=[jax.ShapeDtypeStruct((m, CONV_C), BF16),
                   jax.ShapeDtypeStruct((batch, CONV_K - 1, CONV_C), F32)],
        scratch_shapes=[pltpu.VMEM((NCH, CONV_HIST + CONV_TB, LANE), F32), pltpu.VMEM((NCH, CONV_TB, LANE), F32)],
        compiler_params=_cparams(("parallel", "arbitrary")),
        name="conv_prompt",
    )(u4, u4, u4, _chunked(wpad), _chunked(conv_b.reshape(1, -1)), _chunked(ln_g.reshape(1, -1)),
      _chunked(ln_b.reshape(1, -1)))


def _conv_s_kernel(u4_ref, st_ref, w_ref, cb_ref, lg_ref, lb_ref, yb_ref, so_ref, *, nb):
    ga = u4_ref[:, CONV_C:2 * CONV_C]
    u = ga * jax.nn.sigmoid(u4_ref[:, 2 * CONV_C:3 * CONV_C])
    gate = u4_ref[:, 3 * CONV_C:4 * CONV_C]
    yb_ref[...] = jnp.zeros(yb_ref.shape, yb_ref.dtype)
    for b in range(nb):
        st = st_ref[b]
        un = u[b:b + 1, :]
        y = (jnp.sum(st * w_ref[0:CONV_K - 1, :], axis=0, keepdims=True)
             + un * w_ref[CONV_K - 1:CONV_K, :] + cb_ref[...])
        mu = jnp.mean(y, axis=-1, keepdims=True)
        d = y - mu
        z = d * lax.rsqrt(jnp.mean(d * d, axis=-1, keepdims=True) + EPS) * lg_ref[...] + lb_ref[...]
        g = gate[b:b + 1, :]
        yb_ref[b:b + 1, :] = (z * jax.nn.sigmoid(z) * (g * jax.nn.sigmoid(g))).astype(yb_ref.dtype)
        so_ref[b, 0:CONV_K - 2, :] = st[1:CONV_K - 1, :]
        so_ref[b, CONV_K - 2:CONV_K - 1, :] = un


def _conv_sample(u4s, state, conv_w, conv_b, ln_g, ln_b):
    nb = state.shape[0]
    return pl.pallas_call(
        functools.partial(_conv_s_kernel, nb=nb),
        out_shape=[jax.ShapeDtypeStruct((SAMPLE_ROWS, CONV_C), F32),
                   jax.ShapeDtypeStruct(state.shape, F32)],
        compiler_params=pltpu.CompilerParams(vmem_limit_bytes=VMEM_LIMIT),
        name="conv_sample",
    )(u4s, state, conv_w, conv_b.reshape(1, -1), ln_g.reshape(1, -1), ln_b.reshape(1, -1))


def _attn_s_kernel(q_ref, kn_ref, vn_ref, g_ref, *refs):
    npat = len(DILATED_PAIRS)
    k_refs, v_refs = refs[:npat], refs[npat:2 * npat]
    tab_ref, tnew_ref, o_ref = refs[2 * npat:]
    q = q_ref[...]
    s = [jnp.sum(kr[...] * q[None], axis=-1, keepdims=True) + tab_ref[p]
         for p, kr in enumerate(k_refs)]
    s_new = jnp.sum(q * kn_ref[...], axis=-1, keepdims=True) + tnew_ref[...]
    m = s_new
    for sp in s:
        m = jnp.maximum(m, jnp.max(sp, axis=0))
    p_new = jnp.exp2(s_new - m)
    l = p_new
    acc = p_new * vn_ref[...]
    for sp, vr in zip(s, v_refs):
        p = jnp.exp2(sp - m[None])
        l = l + jnp.sum(p, axis=0)
        acc = acc + jnp.sum(p * vr[...], axis=0)
    g = g_ref[...]
    o_ref[...] = acc * (1.0 / l) * (g * jax.nn.sigmoid(g))


def _attn_sample(qs, kn, vn, gs, cache_k, cache_v, tab_s, tab_new, layer):
    nb = qs.shape[0]
    rows = PATTERN_KEYS
    row = pl.BlockSpec((None, N_HEADS, HEAD_DIM), lambda b: (b, 0, 0))
    views, specs = [], []
    for w, d in DILATED_PAIRS:
        assert (WIN_MAX - w) % (d * rows) == 0
        views.append(lambda c, d=d: c.reshape(DEPTH, nb, WIN_MAX // d, d, N_HEADS, HEAD_DIM))
        specs.append(pl.BlockSpec((None, None, rows, None, N_HEADS, HEAD_DIM),
                                  lambda b, blk=(WIN_MAX - w) // (d * rows): (layer, b, blk, 0, 0, 0)))
    npat = len(DILATED_PAIRS)
    return pl.pallas_call(
        _attn_s_kernel,
        grid=(nb,),
        in_specs=[row, row, row, row, *specs, *specs,
                  pl.BlockSpec((npat, rows, N_HEADS, 1), lambda b: (0, 0, 0, 0)),
                  pl.BlockSpec((N_HEADS, 1), lambda b: (0, 0))],
        out_specs=row,
        out_shape=jax.ShapeDtypeStruct((nb, N_HEADS, HEAD_DIM), F32),
        compiler_params=_cparams(("parallel",)),
        name="attn_sample",
    )(qs, kn, vn, gs, *[v(cache_k) for v in views], *[v(cache_v) for v in views], tab_s, tab_new)


def _rel_bucket(dist):
    max_exact = N_BUCKETS // 2
    df = jnp.maximum(dist, 1).astype(F32)
    large = max_exact + (jnp.log(df / max_exact) / math.log(MAX_DISTANCE / max_exact)
                         * (N_BUCKETS - max_exact)).astype(jnp.int32)
    large = jnp.minimum(large, N_BUCKETS - 1)
    return jnp.where(dist < max_exact, dist, large)


def _bias_tables(rel_bias):
    dist = jnp.arange(WIN_MAX + 1, dtype=jnp.int32)
    bias = rel_bias[_rel_bucket(dist)].astype(F32).T
    mult = sum(((dist % d == 0) & (dist <= w)).astype(F32) for w, d in DILATED_PAIRS)
    t1d = jnp.where(mult > 0, (bias + jnp.log(jnp.maximum(mult, 1.0))) * LOG2E, NEG)
    trow = jnp.full((N_HEADS, 1, TAB_P), NEG, F32).at[:, 0, :WIN_MAX + 1].set(t1d[:, ::-1])
    j = jnp.arange(PATTERN_KEYS, dtype=jnp.int32)
    d_rows = jnp.stack([w - d * j for w, d in DILATED_PAIRS])
    tab_s = (bias[:, d_rows] * LOG2E).transpose(1, 2, 0)[..., None]
    tab_new = ((bias[:, 0] + math.log(len(DILATED_PAIRS))) * LOG2E).reshape(N_HEADS, 1)
    return trow, tab_s, tab_new


def kernel(x_prompt, x_sample, cache_k, cache_v, state_conv, p_prompt, p_sample, rel_bias, g_pre, w_in, conv_w,
           conv_b, ln_g, ln_b, w_out, g_post, w_ple, g_ple, w_pg, b_pg):
    batch, seq, d = x_prompt.shape
    nb = x_sample.shape[0]
    m = batch * seq
    pad = lambda a: jnp.zeros((SAMPLE_ROWS, a.size // nb), a.dtype).at[:nb].set(a.reshape(nb, -1))
    heads = lambda a: a[:nb, :ATT_W].reshape(nb, N_HEADS, HEAD_DIM)

    hp = x_prompt.reshape(m, d)
    hs = pad(x_sample)
    trow, tab_s, tab_new = _bias_tables(rel_bias)
    p_all = p_prompt.reshape(DEPTH, m, PLE_DIM)
    ps_all = jnp.stack([pad(p_sample[l]) for l in range(DEPTH)])
    wple_bf = w_ple.astype(BF16)
    stacks = new_k = new_v = None
    cp_l, cs_l, kn_l, vn_l = [], [], [], []
    for l in range(DEPTH):
        xn = _rmsnorm(hp, g_pre[l], 512)
        xns = _rmsnorm(hs, g_pre[l], SAMPLE_ROWS)
        qkv, qkvs = _mm([xn], [xns], w_in, l, 0, 3 * ATT_W, name="proj_qkv")
        qs = heads(qkvs) * Q_SCALE
        kn_l.append(heads(qkvs[:, ATT_W:]))
        vn_l.append(heads(qkvs[:, 2 * ATT_W:]))
        if l == DEPTH - 1:
            u4, u4s, new_v = _mm([xn], [xns], w_in, l, 3 * ATT_W, ATT_W + 3 * CONV_C, name="proj_rest",
                                 cache=(cache_v, jnp.stack(vn_l)))
            ya, *stacks, new_k = _attn_prompt(qkv, u4, trow, l, batch, seq, stacks=stacks,
                                              cache=(cache_k, jnp.stack(kn_l)))
        else:
            u4, u4s = _mm([xn], [xns], w_in, l, 3 * ATT_W, ATT_W + 3 * CONV_C, name="proj_rest")
            ya, *stacks = _attn_prompt(qkv, u4, trow, l, batch, seq, stacks=stacks)
        yb, cstate = _conv_prompt(u4, conv_w[l], conv_b[l], ln_g[l], ln_b[l], batch, seq)

        yas = _attn_sample(qs, kn_l[l], vn_l[l], heads(u4s), cache_k, cache_v, tab_s, tab_new, l)
        ybs, cstate_s = _conv_sample(u4s, state_conv[l], conv_w[l], conv_b[l], ln_g[l], ln_b[l])

        y, ys = _mm([ya, yb], [pad(yas).astype(BF16), ybs.astype(BF16)], w_out, l, 0, d, name="proj_out")
        h1b, rs, pn = _post(hp, y, p_all, l, g_post[l], wple_bf[l], 512)
        h1bs, rss, pns = _post(hs, ys, ps_all, l, g_post[l], wple_bf[l], SAMPLE_ROWS)
        hp, hs = _gate_mm(h1b, h1bs, w_pg, b_pg, l, g_post[l], wple_bf[l], g_ple[l],
                          (hp, y, rs, pn), (hs, ys, rss, pns))
        cp_l.append(cstate)
        cs_l.append(cstate_s)

    kv_shape = (DEPTH, batch, seq, N_HEADS, HEAD_DIM)
    return (hp.reshape(batch, seq, d), hs[:nb].reshape(nb, 1, d),
            stacks[0].reshape(kv_shape), stacks[1].reshape(kv_shape), jnp.stack(cp_l),
            new_k, new_v, jnp.stack(cs_l))
```
